```python
import math
import jax
import jax.numpy as jnp
from jax import lax
import numpy as np

D_MODEL = 1024
BATCH = 16
SEQ = 4096
DEPTH = 1
DEC_BATCH = 16
DEC_SEQ = 2048
PAST_LEN = 128

ATT_HEADS = 4
ATT_QK_DIM = 64
ATT_V_DIM = 128
ATT_WIDTH = ATT_HEADS * ATT_V_DIM
ATT_IN = 3 * ATT_HEADS * 2 * ATT_QK_DIM
Q_BLOCK = 128
RWKV_HEADS = 8
RWKV_HEAD_DIM = 64
RWKV_WIDTH = RWKV_HEADS * RWKV_HEAD_DIM
LORA_DECAY = 64
LORA_AAA = 64
LORA_GATE = 128
RWKV_IN = 3 * RWKV_WIDTH + 2 * LORA_DECAY + 2 * LORA_AAA + LORA_GATE
IN_WIDTH = ATT_IN + RWKV_IN
MIX_WIDTH = ATT_WIDTH + RWKV_WIDTH
N_GROUPS = 4
EXPERTS_PER_GROUP = 8
N_EXPERTS = N_GROUPS * EXPERTS_PER_GROUP
TOP_K_IN_GROUP = 2
EXPERT_HIDDEN = 512
MOE_BLOCK = 128
ALPHA = (2.0 * DEPTH) ** 0.25
BETA = (8.0 * DEPTH) ** -0.25
LN_EPS = 1e-5
RMS_EPS = 1e-5
RWKV_GN_EPS = 64e-5

kernel_name = "hymba_diffattn_rwkv7_hiermoe_deepnorm"


def _layernorm(x, g, b):
    xf = x.astype(jnp.float32)
    mu = jnp.mean(xf, axis=-1, keepdims=True)
    var = jnp.mean(jnp.square(xf - mu), axis=-1, keepdims=True)
    y = (xf - mu) * lax.rsqrt(var + LN_EPS) * g.astype(jnp.float32) + b.astype(jnp.float32)
    return y.astype(x.dtype)


def _rmsnorm(x, g):
    xf = x.astype(jnp.float32)
    y = xf * lax.rsqrt(jnp.mean(jnp.square(xf), axis=-1, keepdims=True) + RMS_EPS)
    return y * g.astype(jnp.float32)


def _diff_attention(q, k, v, lam, lam_init, subln_g):
    B, S = q.shape[0], q.shape[1]
    nb = S // Q_BLOCK
    slopes = 2.0 ** (-8.0 * jnp.arange(1, ATT_HEADS + 1, dtype=jnp.float32) / ATT_HEADS)
    pos_k = jnp.arange(S, dtype=jnp.float32)
    qb = (q * (ATT_QK_DIM ** -0.5)).reshape(B, nb, Q_BLOCK, ATT_HEADS, 2, ATT_QK_DIM)
    qb = jnp.moveaxis(qb, 1, 0)
    starts = jnp.arange(nb, dtype=jnp.int32) * Q_BLOCK

    def block(args):
        q_blk, start = args
        s = jnp.einsum('bqhcd,bkhcd->bhcqk', q_blk, k).astype(jnp.float32)
        pos_q = start.astype(jnp.float32) + jnp.arange(Q_BLOCK, dtype=jnp.float32)
        dist = jnp.abs(pos_q[:, None] - pos_k[None, :])
        s = s - slopes[:, None, None, None] * dist
        p = jax.nn.softmax(s, axis=-1)
        attn = p[:, :, 0] - lam * p[:, :, 1]
        return jnp.einsum('bhqk,bkhe->bqhe', attn.astype(v.dtype), v)

    o = lax.map(block, (qb, starts))
    o = jnp.moveaxis(o, 0, 1).reshape(B, S, ATT_HEADS, ATT_V_DIM)
    o = _rmsnorm(o, subln_g) * (1.0 - lam_init)
    return o.reshape(B, S, ATT_WIDTH).astype(v.dtype)


def _to_scan(f, b):
    z = jnp.stack([f, b[:, ::-1]], axis=0)
    return jnp.transpose(z, (2, 0, 1, 3, 4)).astype(jnp.float32)


def _rwkv7_step(state, inp):
    r_t, w_t, k_t, v_t, kk_t, kka_t = inp
    sa = jnp.einsum('dbhij,dbhj->dbhi', state, -kk_t)
    state = (state * w_t[..., None, :]
             + sa[..., :, None] * kka_t[..., None, :]
             + v_t[..., :, None] * k_t[..., None, :])
    y = jnp.einsum('dbhij,dbhj->dbhi', state, r_t)
    return state, y


def _rwkv7_bidir(p, mu, w0, w_up, a0, a_up, g_up, k_k, k_a, r_k, ln_g, ln_b):
    B, S, _ = p.shape
    H, N = RWKV_HEADS, RWKV_HEAD_DIM
    prev = jnp.pad(p[:, :-1], ((0, 0), (1, 0), (0, 0)))
    nxt = jnp.pad(p[:, 1:], ((0, 0), (0, 1), (0, 0)))
    p = p + mu[0] * (prev - p) + mu[1] * (nxt - p)
    c0 = RWKV_WIDTH
    r, k, v, wd, ad, gd = jnp.split(
        p, [c0, 2 * c0, 3 * c0, 3 * c0 + 2 * LORA_DECAY, 3 * c0 + 2 * LORA_DECAY + 2 * LORA_AAA], axis=-1)
    wd = wd.reshape(B, S, 2, LORA_DECAY)
    ad = ad.reshape(B, S, 2, LORA_AAA)
    wl = (w0 + jnp.einsum('bsdr,drc->bsdc', jnp.tanh(wd), w_up)).astype(jnp.float32)
    decay = jnp.exp(-jnp.exp(-jax.nn.softplus(-wl) - 0.5))
    a = jax.nn.sigmoid((a0 + jnp.einsum('bsdr,drc->bsdc', ad, a_up)).astype(jnp.float32))
    g = jax.nn.sigmoid(gd) @ g_up
    kk = (k * k_k).reshape(B, S, H, N).astype(jnp.float32)
    kk = kk / jnp.maximum(jnp.sqrt(jnp.sum(jnp.square(kk), axis=-1, keepdims=True)), 1e-12)
    k_dir = k[:, :, None, :].astype(jnp.float32) * (1.0 + (a - 1.0) * k_a.astype(jnp.float32))
    rh = r.reshape(B, S, H, N)
    vh = v.reshape(B, S, H, N)
    dh = decay.reshape(B, S, 2, H, N)
    ah = a.reshape(B, S, 2, H, N)
    kh = k_dir.reshape(B, S, 2, H, N)
    xs = (_to_scan(rh, rh),
          _to_scan(dh[:, :, 0], dh[:, :, 1]),
          _to_scan(kh[:, :, 0], kh[:, :, 1]),
          _to_scan(vh, vh),
          _to_scan(kk, kk),
          _to_scan(kk * ah[:, :, 0], kk * ah[:, :, 1]))
    state0 = jnp.zeros((2, B, H, N, N), jnp.float32)
    _, y = lax.scan(_rwkv7_step, state0, xs)
    y = jnp.transpose(y[:, 0] + y[::-1, 1], (1, 0, 2, 3))
    mu_y = jnp.mean(y, axis=-1, keepdims=True)
    var_y = jnp.mean(jnp.square(y - mu_y), axis=-1, keepdims=True)
    y = (y - mu_y) * lax.rsqrt(var_y + RWKV_GN_EPS)
    y = y * ln_g.reshape(H, N).astype(jnp.float32) + ln_b.reshape(H, N).astype(jnp.float32)
    bonus = jnp.sum(rh.astype(jnp.float32) * (kh[:, :, 0] + kh[:, :, 1]) * r_k.astype(jnp.float32),
                    axis=-1, keepdims=True) * vh.astype(jnp.float32)
    out = (y + bonus).reshape(B, S, RWKV_WIDTH) * g.astype(jnp.float32)
    return out.astype(p.dtype)


def _hier_moe(h, w_group, b_group, w_exp_router, b_exp_router, w_gate, w_up, w_down):
    B, S, D = h.shape
    N = B * S
    A = N * TOP_K_IN_GROUP
    xf = h.reshape(N, D)
    g_logits = (xf @ w_group + b_group).astype(jnp.float32)
    g_prob = jax.nn.softmax(g_logits, axis=-1)
    grp = jnp.argmax(g_logits, axis=-1)
    g1 = jnp.take_along_axis(g_prob, grp[:, None], axis=1)[:, 0]
    e_logits = (jnp.einsum('nd,gde->nge', xf, w_exp_router) + b_exp_router).astype(jnp.float32)
    sel = jnp.take_along_axis(e_logits, grp[:, None, None], axis=1)[:, 0]
    top_v, top_i = lax.top_k(sel, TOP_K_IN_GROUP)
    gates = g1[:, None] * jax.nn.softmax(top_v, axis=-1)
    expert = grp[:, None] * EXPERTS_PER_GROUP + top_i
    e_flat = expert.reshape(A).astype(jnp.int32)
    tok_flat = jnp.repeat(jnp.arange(N, dtype=jnp.int32), TOP_K_IN_GROUP)
    gate_flat = gates.reshape(A)
    order = jnp.argsort(e_flat)
    e_s, tok_s, gate_s = e_flat[order], tok_flat[order], gate_flat[order]
    counts = jax.ops.segment_sum(jnp.ones((A,), jnp.int32), e_flat, num_segments=N_EXPERTS)
    padded = (counts + MOE_BLOCK - 1) // MOE_BLOCK * MOE_BLOCK
    starts = jnp.cumsum(counts) - counts
    pends = jnp.cumsum(padded)
    pstarts = pends - padded
    dest = pstarts[e_s] + jnp.arange(A, dtype=jnp.int32) - starts[e_s]
    NB = A // MOE_BLOCK + N_EXPERTS
    P = NB * MOE_BLOCK
    buf_tok = jnp.full((P,), N, jnp.int32).at[dest].set(tok_s)
    buf_gate = jnp.zeros((P,), jnp.float32).at[dest].set(gate_s)
    block_start = jnp.arange(NB, dtype=jnp.int32) * MOE_BLOCK
    block_exp = jnp.minimum(jnp.sum(block_start[:, None] >= pends[None, :], axis=1), N_EXPERTS - 1)
    x_pad = jnp.concatenate([xf, jnp.zeros((1, D), xf.dtype)], axis=0)
    xb = x_pad[buf_tok].reshape(NB, MOE_BLOCK, D)

    def run(args):
        x_blk, e = args
        hid = jax.nn.silu(x_blk @ w_gate[e]) * (x_blk @ w_up[e])
        return hid @ w_down[e]

    yb = lax.map(run, (xb, block_exp)).reshape(P, D)
    yb = yb * buf_gate[:, None].astype(yb.dtype)
    y = jnp.zeros((N + 1, D), yb.dtype).at[buf_tok].add(yb)[:N]
    return y.reshape(B, S, D)


def _layer(x, layer_idx, w_in, att_lambda, att_subln_g, rwkv_mu, rwkv_w0, rwkv_w_up, rwkv_a0,
           rwkv_a_up, rwkv_g_up, rwkv_k_k, rwkv_k_a, rwkv_r_k, rwkv_ln_g, rwkv_ln_b, w_out,
           ln1_g, ln1_b, router_g_w, router_g_b, router_e_w, router_e_b, exp_w_gate, exp_w_up,
           exp_w_down, ln2_g, ln2_b):
    B, S, _ = x.shape
    proj = x @ w_in
    qa = ATT_HEADS * 2 * ATT_QK_DIM
    q, k, v, rw = jnp.split(proj, [qa, 2 * qa, 3 * qa], axis=-1)
    q = q.reshape(B, S, ATT_HEADS, 2, ATT_QK_DIM)
    k = k.reshape(B, S, ATT_HEADS, 2, ATT_QK_DIM)
    v = v.reshape(B, S, ATT_HEADS, ATT_V_DIM)
    lam_init = 0.8 - 0.6 * math.exp(-0.3 * layer_idx)
    lmb = att_lambda.astype(jnp.float32)
    lam = jnp.exp(jnp.sum(lmb[0] * lmb[1])) - jnp.exp(jnp.sum(lmb[2] * lmb[3])) + lam_init
    att = _diff_attention(q, k, v, lam, lam_init, att_subln_g)
    tm = _rwkv7_bidir(rw, rwkv_mu, rwkv_w0, rwkv_w_up, rwkv_a0, rwkv_a_up, rwkv_g_up,
                      rwkv_k_k, rwkv_k_a, rwkv_r_k, rwkv_ln_g, rwkv_ln_b)
    mix = jnp.concatenate([att, tm], axis=-1) @ w_out
    h = _layernorm(ALPHA * x + mix, ln1_g, ln1_b)
    moe = _hier_moe(h, router_g_w, router_g_b, router_e_w, router_e_b, exp_w_gate, exp_w_up, exp_w_down)
    return _layernorm(ALPHA * h + moe, ln2_g, ln2_b)


def _trunk(x, w_in, att_lambda, att_subln_g, rwkv_mu, rwkv_w0, rwkv_w_up, rwkv_a0, rwkv_a_up,
           rwkv_g_up, rwkv_k_k, rwkv_k_a, rwkv_r_k, rwkv_ln_g, rwkv_ln_b, w_out, ln1_g, ln1_b,
           router_g_w, router_g_b, router_e_w, router_e_b, exp_w_gate, exp_w_up, exp_w_down,
           ln2_g, ln2_b):
    for l in range(DEPTH):
        x = _layer(x, l, w_in[l], att_lambda[l], att_subln_g[l], rwkv_mu[l], rwkv_w0[l],
                   rwkv_w_up[l], rwkv_a0[l], rwkv_a_up[l], rwkv_g_up[l], rwkv_k_k[l], rwkv_k_a[l],
                   rwkv_r_k[l], rwkv_ln_g[l], rwkv_ln_b[l], w_out[l], ln1_g[l], ln1_b[l],
                   router_g_w[l], router_g_b[l], router_e_w[l], router_e_b[l], exp_w_gate[l],
                   exp_w_up[l], exp_w_down[l], ln2_g[l], ln2_b[l])
    return x


def setup_inputs(seed: int = 0) -> dict:
    key = jax.random.key(seed)
    ks = jax.random.split(key, 32)
    f32 = jnp.float32

    def nrm(k, shape, scale):
        return jax.random.normal(k, shape, f32) * scale

    L, D = DEPTH, D_MODEL
    return {
        "x_prompt": nrm(ks[0], (BATCH, SEQ, D), 1.0),
        "x_sample": nrm(ks[1], (DEC_BATCH, DEC_SEQ, D), 1.0),
        "w_in": nrm(ks[2], (L, D, IN_WIDTH), D ** -0.5),
        "att_lambda": nrm(ks[3], (L, 4, ATT_QK_DIM), 0.1),
        "att_subln_g": 1.0 + nrm(ks[4], (L, ATT_V_DIM), 0.02),
        "rwkv_mu": jax.random.uniform(ks[5], (L, 2, RWKV_IN), f32, 0.0, 0.5),
        "rwkv_w0": jax.random.uniform(ks[6], (L, 2, RWKV_WIDTH), f32, -6.0, 2.0),
        "rwkv_w_up": nrm(ks[7], (L, 2, LORA_DECAY, RWKV_WIDTH), 0.5 * LORA_DECAY ** -0.5),
        "rwkv_a0": nrm(ks[8], (L, 2, RWKV_WIDTH), 0.5),
        "rwkv_a_up": nrm(ks[9], (L, 2, LORA_AAA, RWKV_WIDTH), 0.5 * LORA_AAA ** -0.5),
        "rwkv_g_up": nrm(ks[10], (L, LORA_GATE, RWKV_WIDTH), LORA_GATE ** -0.5),
        "rwkv_k_k": 0.85 + nrm(ks[11], (L, RWKV_WIDTH), 0.05),
        "rwkv_k_a": 1.0 + nrm(ks[12], (L, RWKV_WIDTH), 0.05),
        "rwkv_r_k": nrm(ks[13], (L, RWKV_HEADS, RWKV_HEAD_DIM), 0.1),
        "rwkv_ln_g": 1.0 + nrm(ks[14], (L, RWKV_WIDTH), 0.02),
        "rwkv_ln_b": nrm(ks[15], (L, RWKV_WIDTH), 0.02),
        "w_out": nrm(ks[16], (L, MIX_WIDTH, D), BETA * MIX_WIDTH ** -0.5),
        "ln1_g": 1.0 + nrm(ks[17], (L, D), 0.02),
        "ln1_b": nrm(ks[18], (L, D), 0.02),
        "router_g_w": nrm(ks[19], (L, D, N_GROUPS), D ** -0.5),
        "router_g_b": nrm(ks[20], (L, N_GROUPS), 0.01),
        "router_e_w": nrm(ks[21], (L, N_GROUPS, D, EXPERTS_PER_GROUP), D ** -0.5),
        "router_e_b": nrm(ks[22], (L, N_GROUPS, EXPERTS_PER_GROUP), 0.01),
        "exp_w_gate": nrm(ks[23], (L, N_EXPERTS, D, EXPERT_HIDDEN), D ** -0.5),
        "exp_w_up": nrm(ks[24], (L, N_EXPERTS, D, EXPERT_HIDDEN), D ** -0.5),
        "exp_w_down": nrm(ks[25], (L, N_EXPERTS, EXPERT_HIDDEN, D), BETA * EXPERT_HIDDEN ** -0.5),
        "ln2_g": 1.0 + nrm(ks[26], (L, D), 0.02),
        "ln2_b": nrm(ks[27], (L, D), 0.02),
    }


def reference(x_prompt, x_sample, w_in, att_lambda, att_subln_g, rwkv_mu, rwkv_w0, rwkv_w_up,
              rwkv_a0, rwkv_a_up, rwkv_g_up, rwkv_k_k, rwkv_k_a, rwkv_r_k, rwkv_ln_g, rwkv_ln_b,
              w_out, ln1_g, ln1_b, router_g_w, router_g_b, router_e_w, router_e_b, exp_w_gate,
              exp_w_up, exp_w_down, ln2_g, ln2_b):
    y_prompt = _trunk(x_prompt, w_in, att_lambda, att_subln_g, rwkv_mu, rwkv_w0, rwkv_w_up,
                      rwkv_a0, rwkv_a_up, rwkv_g_up, rwkv_k_k, rwkv_k_a, rwkv_r_k, rwkv_ln_g,
                      rwkv_ln_b, w_out, ln1_g, ln1_b, router_g_w, router_g_b, router_e_w,
                      router_e_b, exp_w_gate, exp_w_up, exp_w_down, ln2_g, ln2_b)
    y_sample = _trunk(x_sample, w_in, att_lambda, att_subln_g, rwkv_mu, rwkv_w0, rwkv_w_up,
                      rwkv_a0, rwkv_a_up, rwkv_g_up, rwkv_k_k, rwkv_k_a, rwkv_r_k, rwkv_ln_g,
                      rwkv_ln_b, w_out, ln1_g, ln1_b, router_g_w, router_g_b, router_e_w,
                      router_e_b, exp_w_gate, exp_w_up, exp_w_down, ln2_g, ln2_b)
    return (y_prompt, y_sample)
```

```python
import functools
import math

import jax
import jax.numpy as jnp
from jax import lax
from jax.experimental import pallas as pl
from jax.experimental.pallas import tpu as pltpu

F32 = jnp.float32
BF16 = jnp.bfloat16

D_MODEL = 1024
ATT_HEADS = 4
ATT_QK_DIM = 64
ATT_V_DIM = 128
ATT_WIDTH = ATT_HEADS * ATT_V_DIM
QK_WIDTH = 2 * ATT_HEADS * 2 * ATT_QK_DIM
RWKV_HEADS = 8
RWKV_HEAD_DIM = 64
RWKV_WIDTH = RWKV_HEADS * RWKV_HEAD_DIM
LORA_DECAY = 64
LORA_AAA = 64
LORA_GATE = 128
RWKV_IN = 3 * RWKV_WIDTH + 2 * LORA_DECAY + 2 * LORA_AAA + LORA_GATE
N_GROUPS = 4
EXPERTS_PER_GROUP = 8
N_EXPERTS = N_GROUPS * EXPERTS_PER_GROUP
TOP_K = 2
EXPERT_HIDDEN = 512
MOE_BLOCK = 128
DEPTH = 1
ALPHA = (2.0 * DEPTH) ** 0.25
LN_EPS = 1e-5
RMS_EPS = 1e-5
RWKV_GN_EPS = 64e-5
LAM_INIT = 0.8 - 0.6 * math.exp(-0.3 * 0)

LANES = 128
ROUTE_LANES = 128
VMEM_LIMIT = 56 * 1024 * 1024

PROJ_TILE = 512
ATT_QB = 128
ATT_KB = 512
PRE_TILE = 256
CHUNK = 128
MIX_TILE = 256
OUT_TILE = 256
NEG_BIG = -1e30


def _bdot(a, b):
    return jnp.dot(a.astype(BF16), b.astype(BF16), preferred_element_type=F32)


def _bdot_nt(a, b):
    return lax.dot_general(a.astype(BF16), b.astype(BF16), (((1,), (1,)), ((), ())),
                           preferred_element_type=F32)


def _params(sem):
    return pltpu.CompilerParams(dimension_semantics=sem, vmem_limit_bytes=VMEM_LIMIT)


def _in_proj_kernel(x_ref, wqk_ref, wvt_ref, wrw_ref, qk_ref, vt_ref, rw_ref):
    xb = x_ref[0].astype(BF16)
    qk_ref[0] = jnp.dot(xb, wqk_ref[...], preferred_element_type=F32).astype(BF16)
    vt_ref[0] = lax.dot_general(wvt_ref[...], xb, (((1,), (1,)), ((), ())),
                                preferred_element_type=F32).astype(BF16)
    rw_ref[0] = jnp.dot(xb, wrw_ref[...], preferred_element_type=F32)


def _in_proj(x, wqk, wvt, wrw):
    B, S, D = x.shape
    T = PROJ_TILE
    return pl.pallas_call(
        _in_proj_kernel,
        grid=(B, S // T),
        in_specs=[
            pl.BlockSpec((1, T, D), lambda b, i: (b, i, 0)),
            pl.BlockSpec((D, QK_WIDTH), lambda b, i: (0, 0)),
            pl.BlockSpec((ATT_WIDTH, D), lambda b, i: (0, 0)),
            pl.BlockSpec((D, RWKV_IN), lambda b, i: (0, 0)),
        ],
        out_specs=[
            pl.BlockSpec((1, T, QK_WIDTH), lambda b, i: (b, i, 0)),
            pl.BlockSpec((1, ATT_WIDTH, T), lambda b, i: (b, 0, i)),
            pl.BlockSpec((1, T, RWKV_IN), lambda b, i: (b, i, 0)),
        ],
        out_shape=[
            jax.ShapeDtypeStruct((B, S, QK_WIDTH), BF16),
            jax.ShapeDtypeStruct((B, ATT_WIDTH, S), BF16),
            jax.ShapeDtypeStruct((B, S, RWKV_IN), F32),
        ],
        compiler_params=_params(("parallel", "parallel")),
        name="in_proj",
    )(x, wqk, wvt, wrw)


def _attention_kernel(slopes_ref, q_ref, k_ref, vt_ref, lam_ref, g_ref, o_ref, *, seq):
    h = pl.program_id(1)
    qi = pl.program_id(2)
    QB, KB = ATT_QB, ATT_KB
    slope = slopes_ref[h]

    q = q_ref[0] * jnp.asarray(ATT_QK_DIM ** -0.5, BF16)
    lane = lax.broadcasted_iota(jnp.int32, (QB, LANES), 1)
    zero = jnp.zeros_like(q)
    qq = jnp.concatenate([jnp.where(lane < ATT_QK_DIM, q, zero),
                          jnp.where(lane >= ATT_QK_DIM, q, zero)], axis=0)

    kpos0 = lax.broadcasted_iota(jnp.int32, (KB, QB), 0)
    qpos = lax.broadcasted_iota(jnp.int32, (KB, QB), 1) + qi * QB
    rel = kpos0 - qpos

    def body(j, carry):
        m, l, acc = carry
        k0 = pl.multiple_of(j * KB, KB)
        kt = k_ref[0, pl.ds(k0, KB), :]
        s = lax.dot_general(kt, qq, (((1,), (1,)), ((), ())), preferred_element_type=F32)
        bias = slope * jnp.abs(rel + k0).astype(F32)
        s = s - jnp.concatenate([bias, bias], axis=1)
        m_new = jnp.maximum(m, jnp.max(s, axis=0, keepdims=True))
        alpha = jnp.exp(m - m_new)
        p = jnp.exp(s - m_new)
        l = alpha * l + jnp.sum(p, axis=0, keepdims=True)
        vt = vt_ref[0, :, pl.ds(k0, KB)]
        acc = acc * alpha + jnp.dot(vt, p.astype(BF16), preferred_element_type=F32)
        return m_new, l, acc

    m0 = jnp.full((1, 2 * QB), NEG_BIG, F32)
    l0 = jnp.zeros((1, 2 * QB), F32)
    a0 = jnp.zeros((ATT_V_DIM, 2 * QB), F32)
    m, l, acc = lax.fori_loop(0, seq // KB, body, (m0, l0, a0))

    lmb = lam_ref[...]
    lam = (jnp.exp(jnp.sum(lmb[0:1] * lmb[1:2], axis=1, keepdims=True))
           - jnp.exp(jnp.sum(lmb[2:3] * lmb[3:4], axis=1, keepdims=True)) + LAM_INIT)
    o = acc[:, :QB] / l[:, :QB] - lam * (acc[:, QB:] / l[:, QB:])
    ms = jnp.mean(o * o, axis=0, keepdims=True)
    o = o * lax.rsqrt(ms + RMS_EPS) * g_ref[...] * (1.0 - LAM_INIT)
    o_ref[0] = o.T.astype(BF16)


def _attention(qk, vt, att_lambda, subln_g):
    B, S, _ = qk.shape
    H = ATT_HEADS
    slopes = jnp.asarray([2.0 ** (-8.0 * (i + 1) / H) for i in range(H)], F32)
    return pl.pallas_call(
        functools.partial(_attention_kernel, seq=S),
        grid_spec=pltpu.PrefetchScalarGridSpec(
            num_scalar_prefetch=1,
            grid=(B, H, S // ATT_QB),
            in_specs=[
                pl.BlockSpec((1, ATT_QB, LANES), lambda b, h, i, s: (b, i, h)),
                pl.BlockSpec((1, S, LANES), lambda b, h, i, s: (b, 0, H + h)),
                pl.BlockSpec((1, ATT_V_DIM, S), lambda b, h, i, s: (b, h, 0)),
                pl.BlockSpec((4, ATT_QK_DIM), lambda b, h, i, s: (0, 0)),
                pl.BlockSpec((ATT_V_DIM, 1), lambda b, h, i, s: (0, 0)),
            ],
            out_specs=pl.BlockSpec((1, ATT_QB, LANES), lambda b, h, i, s: (b, i, h)),
        ),
        out_shape=jax.ShapeDtypeStruct((B, S, ATT_WIDTH), BF16),
        compiler_params=_params(("parallel", "parallel", "parallel")),
        name="attention",
    )(slopes, qk, qk, vt, att_lambda, subln_g.reshape(ATT_V_DIM, 1))


def _head_sum(x, blk):
    return jnp.dot(x, blk, preferred_element_type=F32, precision=lax.Precision.HIGHEST)


def _rwkv_pre_kernel(p_ref, prev_ref, next_ref, mu_ref, w0_ref, wup_ref, a0_ref, aup_ref, gup_ref,
                     kk_ref, ka_ref, rk_ref, blk_ref,
                     r_ref, a_ref, lw_ref, kd_ref, kka_ref, vt_ref, g_ref, bonus_ref):
    i = pl.program_id(1)
    n = pl.num_programs(1)
    T = PRE_TILE
    W = RWKV_WIDTH
    p = p_ref[0]
    row = lax.broadcasted_iota(jnp.int32, p.shape, 0)
    prev_row = jnp.where(i > 0, prev_ref[0, 7:8, :], 0.0)
    next_row = jnp.where(i < n - 1, next_ref[0, 0:1, :], 0.0)
    prev = jnp.where(row == 0, prev_row, pltpu.roll(p, 1, axis=0))
    nxt = jnp.where(row == T - 1, next_row, pltpu.roll(p, T - 1, axis=0))
    p = p + mu_ref[0:1, :] * (prev - p) + mu_ref[1:2, :] * (nxt - p)

    r = p[:, 0:W]
    k = p[:, W:2 * W]
    v = p[:, 2 * W:3 * W]
    c = 3 * W
    wd = p[:, c:c + 2 * LORA_DECAY]
    ad = p[:, c + 2 * LORA_DECAY:c + 2 * LORA_DECAY + 2 * LORA_AAA]
    gd = p[:, c + 2 * LORA_DECAY + 2 * LORA_AAA:]

    wl = w0_ref[...] + _bdot(jnp.tanh(wd), wup_ref[...])
    lw = -math.exp(-0.5) * jax.nn.sigmoid(wl)
    av = jax.nn.sigmoid(a0_ref[...] + _bdot(ad, aup_ref[...]))
    g_ref[0] = _bdot(jax.nn.sigmoid(gd), gup_ref[...])

    blk = blk_ref[...]
    kkr = k * kk_ref[...]
    kk = kkr / jnp.maximum(jnp.sqrt(_head_sum(kkr * kkr, blk)), 1e-12)
    ksum = jnp.zeros_like(k)
    for d in range(2):
        a_d = av[:, d * W:(d + 1) * W]
        k_d = k * (1.0 + (a_d - 1.0) * ka_ref[...])
        ksum = ksum + k_d
        lw_ref[d, 0] = lw[:, d * W:(d + 1) * W]
        kd_ref[d, 0] = k_d
        kka_ref[d, 0] = kk * a_d
    r_ref[0] = r
    a_ref[0] = -kk
    vt_ref[0] = v.T
    bonus_ref[0] = _head_sum(r * ksum * rk_ref[...], blk) * v


def _rwkv_pre(rw, mu, w0cat, wup_blk, a0cat, aup_blk, gup, k_k, k_a, r_k, blk):
    B, S, _ = rw.shape
    T = PRE_TILE
    W = RWKV_WIDTH
    nb8 = S // 8
    full = lambda shape: pl.BlockSpec(shape, lambda b, i: (0,) * len(shape))
    row_spec = pl.BlockSpec((1, T, W), lambda b, i: (b, i, 0))
    dir_spec = pl.BlockSpec((2, 1, T, W), lambda b, i: (0, b, i, 0))
    row_shape = jax.ShapeDtypeStruct((B, S, W), F32)
    dir_shape = jax.ShapeDtypeStruct((2, B, S, W), F32)
    return pl.pallas_call(
        _rwkv_pre_kernel,
        grid=(B, S // T),
        in_specs=[
            pl.BlockSpec((1, T, RWKV_IN), lambda b, i: (b, i, 0)),
            pl.BlockSpec((1, 8, RWKV_IN), lambda b, i: (b, jnp.maximum(i * (T // 8) - 1, 0), 0)),
            pl.BlockSpec((1, 8, RWKV_IN), lambda b, i: (b, jnp.minimum((i + 1) * (T // 8), nb8 - 1), 0)),
            full((2, RWKV_IN)), full((1, 2 * W)), full((2 * LORA_DECAY, 2 * W)),
            full((1, 2 * W)), full((2 * LORA_AAA, 2 * W)), full((LORA_GATE, W)),
            full((1, W)), full((1, W)), full((1, W)), full((W, W)),
        ],
        out_specs=[row_spec, row_spec, dir_spec, dir_spec, dir_spec,
                   pl.BlockSpec((1, W, T), lambda b, i: (b, 0, i)), row_spec, row_spec],
        out_shape=[row_shape, row_shape, dir_shape, dir_shape, dir_shape,
                   jax.ShapeDtypeStruct((B, W, S), F32), row_shape, row_shape],
        compiler_params=_params(("parallel", "parallel")),
        name="rwkv_pre",
    )(rw, rw, rw, mu, w0cat, wup_blk, a0cat, aup_blk, gup, k_k, k_a, r_k, blk)


def _rwkv_scan_kernel(r_ref, a_ref, lw_ref, kd_ref, kka_ref, vt_ref, yt_ref, state_ref):
    d = pl.program_id(1)
    c = pl.program_id(2)
    C = CHUNK
    N = RWKV_HEAD_DIM

    @pl.when(c == 0)
    def _():
        state_ref[...] = jnp.zeros_like(state_ref)

    row = lax.broadcasted_iota(jnp.int32, (C, C), 0)
    col = lax.broadcasted_iota(jnp.int32, (C, C), 1)
    order = (row - col) * (1 - 2 * d)
    strict = order > 0
    incl = order >= 0
    tri = incl.astype(F32)
    eye = (row == col).astype(F32)

    for h in range(RWKV_HEADS):
        sl = slice(h * N, (h + 1) * N)
        r = r_ref[0, :, sl]
        a = a_ref[0, :, sl]
        lw = lw_ref[0, 0, :, sl]
        k = kd_ref[0, 0, :, sl]
        b = kka_ref[0, 0, :, sl]
        vt = vt_ref[0, sl, :]
        s0 = state_ref[h]

        cs = jnp.dot(tri, lw, preferred_element_type=F32, precision=lax.Precision.HIGHEST)
        tot = jnp.sum(lw, axis=0, keepdims=True)
        g_inv = jnp.exp(-cs)
        g_last = jnp.exp(tot - cs)
        a_t = a * jnp.exp(cs - lw)
        r_t = r * jnp.exp(cs)
        gram = _bdot_nt(jnp.concatenate([a_t, r_t], axis=0),
                        jnp.concatenate([b * g_inv, k * g_inv], axis=0))
        l_ab = jnp.where(strict, gram[:C, :C], 0.0)
        l_ak = jnp.where(strict, gram[:C, C:], 0.0)
        m_rb = jnp.where(incl, gram[C:, :C], 0.0)
        m_rk = jnp.where(incl, gram[C:, C:], 0.0)

        t = eye + l_ab
        pw = l_ab
        for _ in range(int(math.log2(C)) - 1):
            pw = _bdot(pw, pw)
            t = t + _bdot(t, pw)

        w = _bdot(t, a_t)
        x = _bdot(t, l_ak)
        ut = _bdot_nt(s0, w) + _bdot_nt(vt, x)
        yt = _bdot_nt(s0, r_t) + _bdot_nt(ut, m_rb) + _bdot_nt(vt, m_rk)
        yt_ref[0, 0, sl, :] = yt
        state_ref[h] = s0 * jnp.exp(tot) + _bdot(ut, b * g_last) + _bdot(vt, k * g_last)


def _rwkv_scan(r, a, lw, kd, kka, vt):
    B, S, W = r.shape
    C = CHUNK
    nc = S // C
    cidx = lambda d, c: c + d * (nc - 1 - 2 * c)
    row_spec = pl.BlockSpec((1, C, W), lambda b, d, c: (b, cidx(d, c), 0))
    dir_spec = pl.BlockSpec((1, 1, C, W), lambda b, d, c: (d, b, cidx(d, c), 0))
    return pl.pallas_call(
        _rwkv_scan_kernel,
        grid=(B, 2, nc),
        in_specs=[row_spec, row_spec, dir_spec, dir_spec, dir_spec,
                  pl.BlockSpec((1, W, C), lambda b, d, c: (b, 0, cidx(d, c)))],
        out_specs=pl.BlockSpec((1, 1, W, C), lambda b, d, c: (d, b, 0, cidx(d, c))),
        out_shape=jax.ShapeDtypeStruct((2, B, W, S), F32),
        scratch_shapes=[pltpu.VMEM((RWKV_HEADS, RWKV_HEAD_DIM, RWKV_HEAD_DIM), F32)],
        compiler_params=_params(("parallel", "parallel", "arbitrary")),
        name="rwkv_scan",
    )(r, a, lw, kd, kka, vt)


def _layernorm(z, g, b):
    mu = jnp.mean(z, axis=-1, keepdims=True)
    zc = z - mu
    var = jnp.mean(zc * zc, axis=-1, keepdims=True)
    return zc * lax.rsqrt(var + LN_EPS) * g + b


def _first_lane_where(mask, lane):
    return jnp.min(jnp.where(mask, lane, ROUTE_LANES), axis=1, keepdims=True)


def _mix_ln1_kernel(x_ref, att_ref, ytf_ref, ytb_ref, g_ref, bonus_ref, gng_ref, gnb_ref,
                    woa_ref, wob_ref, l1g_ref, l1b_ref, wr_ref, br_ref,
                    h_ref, exp_ref, gate_ref):
    T = MIX_TILE
    N = RWKV_HEAD_DIM
    yt = ytf_ref[0, 0] + ytb_ref[0, 0]
    y3 = yt.reshape(RWKV_HEADS, N, T)
    mu = jnp.mean(y3, axis=1, keepdims=True)
    yc = y3 - mu
    var = jnp.mean(yc * yc, axis=1, keepdims=True)
    yn = (yc * lax.rsqrt(var + RWKV_GN_EPS)).reshape(RWKV_WIDTH, T)
    yn = yn * gng_ref[...] + gnb_ref[...]
    tm = (yn.T + bonus_ref[0]) * g_ref[0]
    mix = (jnp.dot(att_ref[0], woa_ref[...], preferred_element_type=F32)
           + _bdot(tm, wob_ref[...]))
    h = _layernorm(ALPHA * x_ref[0] + mix, l1g_ref[...], l1b_ref[...])
    h_ref[0] = h

    logits = jnp.dot(h, wr_ref[...], preferred_element_type=F32,
                     precision=lax.Precision.HIGHEST) + br_ref[...]
    lane = lax.broadcasted_iota(jnp.int32, (T, ROUTE_LANES), 1)
    neg = jnp.asarray(-jnp.inf, F32)
    gl = jnp.where(lane < N_GROUPS, logits, neg)
    gmax = jnp.max(gl, axis=1, keepdims=True)
    grp = _first_lane_where(gl == gmax, lane)
    g1 = 1.0 / jnp.sum(jnp.exp(gl - gmax), axis=1, keepdims=True)
    lo = N_GROUPS + EXPERTS_PER_GROUP * grp
    sel = jnp.where((lane >= lo) & (lane < lo + EXPERTS_PER_GROUP), logits, neg)
    v1 = jnp.max(sel, axis=1, keepdims=True)
    i1 = _first_lane_where(sel == v1, lane)
    sel2 = jnp.where(lane == i1, neg, sel)
    v2 = jnp.max(sel2, axis=1, keepdims=True)
    i2 = _first_lane_where(sel2 == v2, lane)
    e2 = jnp.exp(v2 - v1)
    den = 1.0 + e2
    gate1 = g1 * (1.0 / den)
    gate2 = g1 * (e2 / den)
    exp_ref[0] = jnp.where(lane == 0, i1 - N_GROUPS, jnp.where(lane == 1, i2 - N_GROUPS, 0))
    gate_ref[0] = jnp.where(lane == 0, gate1, jnp.where(lane == 1, gate2, 0.0))


def _mix_ln1(x, att, yt, g, bonus, gn_g, gn_b, woa, wob, l1g, l1b, wr, br):
    B, S, D = x.shape
    T = MIX_TILE
    W = RWKV_WIDTH
    full = lambda shape: pl.BlockSpec(shape, lambda b, i: (0,) * len(shape))
    row = lambda width: pl.BlockSpec((1, T, width), lambda b, i: (b, i, 0))
    return pl.pallas_call(
        _mix_ln1_kernel,
        grid=(B, S // T),
        in_specs=[
            row(D), row(ATT_WIDTH),
            pl.BlockSpec((1, 1, W, T), lambda b, i: (0, b, 0, i)),
            pl.BlockSpec((1, 1, W, T), lambda b, i: (1, b, 0, i)),
            row(W), row(W), full((W, 1)), full((W, 1)),
            full((ATT_WIDTH, D)), full((W, D)), full((1, D)), full((1, D)),
            full((D, ROUTE_LANES)), full((1, ROUTE_LANES)),
        ],
        out_specs=[row(D), row(ROUTE_LANES), row(ROUTE_LANES)],
        out_shape=[jax.ShapeDtypeStruct((B, S, D), F32),
                   jax.ShapeDtypeStruct((B, S, ROUTE_LANES), jnp.int32),
                   jax.ShapeDtypeStruct((B, S, ROUTE_LANES), F32)],
        compiler_params=_params(("parallel", "parallel")),
        name="mix_ln1",
    )(x, att, yt, yt, g, bonus, gn_g, gn_b, woa, wob, l1g, l1b, wr, br)


def _start_row_gather(idx_ref, src_hbm, dst, sem, rows):
    def issue(r, carry):
        pltpu.make_async_copy(src_hbm.at[pl.ds(idx_ref[0, 0, r], 1), :],
                              dst.at[pl.ds(r, 1), :], sem).start()
        return carry
    lax.fori_loop(0, rows, issue, 0, unroll=8)


def _wait_row_gather(src_hbm, dst, sem, rows):
    pltpu.make_async_copy(src_hbm.at[pl.ds(0, rows), :], dst, sem).wait()


def _experts_kernel(bexp_ref, tok_ref, tok_next_ref, h_hbm, wg_ref, wu_ref, wd_ref, y_ref,
                    xbuf, sem):
    i = pl.program_id(0)
    n = pl.num_programs(0)
    R = MOE_BLOCK
    slot = i % 2

    @pl.when(i == 0)
    def _():
        _start_row_gather(tok_ref, h_hbm, xbuf.at[0], sem.at[0], R)

    @pl.when(i + 1 < n)
    def _():
        _start_row_gather(tok_next_ref, h_hbm, xbuf.at[1 - slot], sem.at[1 - slot], R)

    _wait_row_gather(h_hbm, xbuf.at[slot], sem.at[slot], R)
    xb = xbuf[slot].astype(BF16)
    hid = (jax.nn.silu(jnp.dot(xb, wg_ref[0], preferred_element_type=F32))
           * jnp.dot(xb, wu_ref[0], preferred_element_type=F32))
    y_ref[...] = _bdot(hid, wd_ref[0])


def _experts(h_flat, buf_tok, block_exp, wg, wu, wd):
    D = D_MODEL
    R = MOE_BLOCK
    nb = block_exp.shape[0]
    tok3 = buf_tok.reshape(nb, 1, R)
    idx_spec = lambda f: pl.BlockSpec((1, 1, R), f, memory_space=pltpu.SMEM)
    return pl.pallas_call(
        _experts_kernel,
        grid_spec=pltpu.PrefetchScalarGridSpec(
            num_scalar_prefetch=1,
            grid=(nb,),
            in_specs=[
                idx_spec(lambda i, e: (i, 0, 0)),
                idx_spec(lambda i, e: (jnp.minimum(i + 1, nb - 1), 0, 0)),
                pl.BlockSpec(memory_space=pl.ANY),
                pl.BlockSpec((1, D, EXPERT_HIDDEN), lambda i, e: (e[i], 0, 0)),
                pl.BlockSpec((1, D, EXPERT_HIDDEN), lambda i, e: (e[i], 0, 0)),
                pl.BlockSpec((1, EXPERT_HIDDEN, D), lambda i, e: (e[i], 0, 0)),
            ],
            out_specs=pl.BlockSpec((R, D), lambda i, e: (i, 0)),
            scratch_shapes=[pltpu.VMEM((2, R, D), F32), pltpu.SemaphoreType.DMA((2,))],
        ),
        out_shape=jax.ShapeDtypeStruct((nb * R, D), F32),
        compiler_params=_params(("arbitrary",)),
        name="experts",
    )(block_exp, tok3, tok3, h_flat, wg, wu, wd)


def _combine_ln2_kernel(pos_ref, pos_next_ref, h_ref, gate_ref, y_hbm, l2g_ref, l2b_ref, o_ref,
                        ybuf, sem):
    i = pl.program_id(0)
    n = pl.num_programs(0)
    T = OUT_TILE
    R = TOP_K * T
    slot = i % 2

    @pl.when(i == 0)
    def _():
        _start_row_gather(pos_ref, y_hbm, ybuf.at[0], sem.at[0], R)

    @pl.when(i + 1 < n)
    def _():
        _start_row_gather(pos_next_ref, y_hbm, ybuf.at[1 - slot], sem.at[1 - slot], R)

    _wait_row_gather(y_hbm, ybuf.at[slot], sem.at[slot], R)
    gates = gate_ref[...]
    moe = ybuf[slot, 0:T, :] * gates[:, 0:1] + ybuf[slot, T:R, :] * gates[:, 1:2]
    o_ref[...] = _layernorm(ALPHA * h_ref[...] + moe, l2g_ref[...], l2b_ref[...])


def _combine_ln2(h_flat, gates, pos, yb, l2g, l2b):
    n_tok, D = h_flat.shape
    T = OUT_TILE
    nt = n_tok // T
    idx_spec = lambda f: pl.BlockSpec((1, 1, TOP_K * T), f, memory_space=pltpu.SMEM)
    return pl.pallas_call(
        _combine_ln2_kernel,
        grid=(nt,),
        in_specs=[
            idx_spec(lambda i: (i, 0, 0)),
            idx_spec(lambda i: (jnp.minimum(i + 1, nt - 1), 0, 0)),
            pl.BlockSpec((T, D), lambda i: (i, 0)),
            pl.BlockSpec((T, ROUTE_LANES), lambda i: (i, 0)),
            pl.BlockSpec(memory_space=pl.ANY),
            pl.BlockSpec((1, D), lambda i: (0, 0)),
            pl.BlockSpec((1, D), lambda i: (0, 0)),
        ],
        out_specs=pl.BlockSpec((T, D), lambda i: (i, 0)),
        out_shape=jax.ShapeDtypeStruct((n_tok, D), F32),
        scratch_shapes=[pltpu.VMEM((2, TOP_K * T, D), F32), pltpu.SemaphoreType.DMA((2,))],
        compiler_params=_params(("arbitrary",)),
        name="combine_ln2",
    )(pos, pos, h_flat, gates, yb, l2g, l2b)


def _dispatch_plan(expert):
    n_tok = expert.shape[0]
    A = n_tok * TOP_K
    e_flat = expert.reshape(A)
    tok_flat = jnp.repeat(jnp.arange(n_tok, dtype=jnp.int32), TOP_K)
    order = jnp.argsort(e_flat)
    e_s, tok_s = e_flat[order], tok_flat[order]
    counts = jnp.zeros((N_EXPERTS,), jnp.int32).at[e_flat].add(1)
    padded = (counts + MOE_BLOCK - 1) // MOE_BLOCK * MOE_BLOCK
    starts = jnp.cumsum(counts) - counts
    pends = jnp.cumsum(padded)
    pstarts = pends - padded
    dest = pstarts[e_s] + jnp.arange(A, dtype=jnp.int32) - starts[e_s]
    nb = A // MOE_BLOCK + N_EXPERTS
    buf_tok = jnp.zeros((nb * MOE_BLOCK,), jnp.int32).at[dest].set(tok_s)
    block_start = jnp.arange(nb, dtype=jnp.int32) * MOE_BLOCK
    block_exp = jnp.minimum(jnp.sum(block_start[:, None] >= pends[None, :], axis=1),
                            N_EXPERTS - 1).astype(jnp.int32)
    pos = jnp.zeros((A,), jnp.int32).at[order].set(dest).reshape(n_tok, TOP_K)
    return buf_tok, block_exp, pos


def _trunk(x, w):
    B, S, D = x.shape
    qk, vt, rw = _in_proj(x, w["wqk"], w["wvt"], w["wrw"])
    att = _attention(qk, vt, w["att_lambda"], w["subln_g"])
    r, a, lw, kd, kka, rvt, g, bonus = _rwkv_pre(
        rw, w["mu"], w["w0cat"], w["wup_blk"], w["a0cat"], w["aup_blk"], w["gup"],
        w["k_k"], w["k_a"], w["r_k"], w["blk"])
    yt = _rwkv_scan(r, a, lw, kd, kka, rvt)
    h, expert, gates = _mix_ln1(x, att, yt, g, bonus, w["gn_g"], w["gn_b"], w["woa"], w["wob"],
                                w["l1g"], w["l1b"], w["wr"], w["br"])
    n_tok = B * S
    h_flat = h.reshape(n_tok, D)
    buf_tok, block_exp, pos = _dispatch_plan(expert.reshape(n_tok, ROUTE_LANES)[:, :TOP_K])
    yb = _experts(h_flat, buf_tok, block_exp, w["wg"], w["wu"], w["wd"])
    T = OUT_TILE
    pos_tiles = pos.reshape(n_tok // T, T, TOP_K).transpose(0, 2, 1).reshape(n_tok // T, 1, TOP_K * T)
    out = _combine_ln2(h_flat, gates.reshape(n_tok, ROUTE_LANES), pos_tiles, yb, w["l2g"], w["l2b"])
    return out.reshape(B, S, D)


def _block_diag2(m):
    z = jnp.zeros_like(m[0])
    return jnp.concatenate([jnp.concatenate([m[0], z], axis=1),
                            jnp.concatenate([z, m[1]], axis=1)], axis=0)


def kernel(x_prompt, x_sample, w_in, att_lambda, att_subln_g, rwkv_mu, rwkv_w0, rwkv_w_up, rwkv_a0, rwkv_a_up, rwkv_g_up, rwkv_k_k, rwkv_k_a, rwkv_r_k, rwkv_ln_g, rwkv_ln_b, w_out, ln1_g, ln1_b, router_g_w, router_g_b, router_e_w, router_e_b, exp_w_gate, exp_w_up, exp_w_down, ln2_g, ln2_b):
    assert w_in.shape[0] == DEPTH
    W = RWKV_WIDTH
    D = D_MODEL
    win = w_in[0]
    head = jnp.arange(W, dtype=jnp.int32) // RWKV_HEAD_DIM
    n_route = N_GROUPS + N_EXPERTS
    wr = jnp.concatenate([router_g_w[0],
                          jnp.transpose(router_e_w[0], (1, 0, 2)).reshape(D, N_EXPERTS)], axis=1)
    br = jnp.concatenate([router_g_b[0], router_e_b[0].reshape(N_EXPERTS)])
    w = {
        "wqk": win[:, :QK_WIDTH].astype(BF16),
        "wvt": win[:, QK_WIDTH:QK_WIDTH + ATT_WIDTH].T.astype(BF16),
        "wrw": win[:, QK_WIDTH + ATT_WIDTH:].astype(BF16),
        "att_lambda": att_lambda[0],
        "subln_g": att_subln_g[0],
        "mu": rwkv_mu[0],
        "w0cat": rwkv_w0[0].reshape(1, 2 * W),
        "wup_blk": _block_diag2(rwkv_w_up[0]).astype(BF16),
        "a0cat": rwkv_a0[0].reshape(1, 2 * W),
        "aup_blk": _block_diag2(rwkv_a_up[0]).astype(BF16),
        "gup": rwkv_g_up[0].astype(BF16),
        "k_k": rwkv_k_k[0].reshape(1, W),
        "k_a": rwkv_k_a[0].reshape(1, W),
        "r_k": rwkv_r_k[0].reshape(1, W),
        "blk": (head[:, None] == head[None, :]).astype(F32),
        "gn_g": rwkv_ln_g[0].reshape(W, 1),
        "gn_b": rwkv_ln_b[0].reshape(W, 1),
        "woa": w_out[0][:ATT_WIDTH].astype(BF16),
        "wob": w_out[0][ATT_WIDTH:].astype(BF16),
        "l1g": ln1_g[0].reshape(1, D),
        "l1b": ln1_b[0].reshape(1, D),
        "wr": jnp.pad(wr, ((0, 0), (0, ROUTE_LANES - n_route))),
        "br": jnp.pad(br, (0, ROUTE_LANES - n_route)).reshape(1, ROUTE_LANES),
        "wg": exp_w_gate[0].astype(BF16),
        "wu": exp_w_up[0].astype(BF16),
        "wd": exp_w_down[0].astype(BF16),
        "l2g": ln2_g[0].reshape(1, D),
        "l2b": ln2_b[0].reshape(1, D),
    }
    return (_trunk(x_prompt, w), _trunk(x_sample, w))
```

```python
import functools
import math

import jax
import jax.numpy as jnp
from jax import lax
from jax.experimental import pallas as pl
from jax.experimental.pallas import tpu as pltpu

F32 = jnp.float32
BF16 = jnp.bfloat16

D_MODEL = 1024
ATT_HEADS = 4
ATT_QK_DIM = 64
ATT_V_DIM = 128
ATT_WIDTH = ATT_HEADS * ATT_V_DIM
QK_WIDTH = 2 * ATT_HEADS * 2 * ATT_QK_DIM
RWKV_HEADS = 8
RWKV_HEAD_DIM = 64
RWKV_WIDTH = RWKV_HEADS * RWKV_HEAD_DIM
LORA_DECAY = 64
LORA_AAA = 64
LORA_GATE = 128
RWKV_IN = 3 * RWKV_WIDTH + 2 * LORA_DECAY + 2 * LORA_AAA + LORA_GATE
N_GROUPS = 4
EXPERTS_PER_GROUP = 8
N_EXPERTS = N_GROUPS * EXPERTS_PER_GROUP
TOP_K = 2
EXPERT_HIDDEN = 512
MOE_BLOCK = 128
DEPTH = 1
ALPHA = (2.0 * DEPTH) ** 0.25
LN_EPS = 1e-5
RMS_EPS = 1e-5
RWKV_GN_EPS = 64e-5
LAM_INIT = 0.8 - 0.6 * math.exp(-0.3 * 0)

LANES = 128
ROUTE_LANES = 128
VMEM_LIMIT = 56 * 1024 * 1024

PROJ_TILE = 512
ATT_QB = 128
ATT_KB = 512
PRE_TILE = 256
CHUNK = 128
MIX_TILE = 256
OUT_TILE = 256
NEG_BIG = -1e30
LOG2E = math.log2(math.e)
Q_SCALE = ATT_QK_DIM ** -0.5 * LOG2E


def _bdot(a, b):
    return jnp.dot(a.astype(BF16), b.astype(BF16), preferred_element_type=F32)


def _bdot_nt(a, b):
    return lax.dot_general(a.astype(BF16), b.astype(BF16), (((1,), (1,)), ((), ())),
                           preferred_element_type=F32)


def _params(sem):
    return pltpu.CompilerParams(dimension_semantics=sem, vmem_limit_bytes=VMEM_LIMIT)


def _in_proj_kernel(x_ref, wqk_ref, wvt_ref, wrw_ref, qk_ref, vt_ref, rw_ref):
    xb = x_ref[0].astype(BF16)
    qk = jnp.dot(xb, wqk_ref[...], preferred_element_type=F32)
    qk_ref[0, :, :QK_WIDTH // 2] = (qk[:, :QK_WIDTH // 2] * Q_SCALE).astype(BF16)
    qk_ref[0, :, QK_WIDTH // 2:] = qk[:, QK_WIDTH // 2:].astype(BF16)
    vt_ref[0] = lax.dot_general(wvt_ref[...], xb, (((1,), (1,)), ((), ())),
                                preferred_element_type=F32).astype(BF16)
    rw_ref[0] = jnp.dot(xb, wrw_ref[...], preferred_element_type=F32)


def _in_proj(x, wqk, wvt, wrw):
    B, S, D = x.shape
    T = PROJ_TILE
    return pl.pallas_call(
        _in_proj_kernel,
        grid=(B, S // T),
        in_specs=[
            pl.BlockSpec((1, T, D), lambda b, i: (b, i, 0)),
            pl.BlockSpec((D, QK_WIDTH), lambda b, i: (0, 0)),
            pl.BlockSpec((ATT_WIDTH, D), lambda b, i: (0, 0)),
            pl.BlockSpec((D, RWKV_IN), lambda b, i: (0, 0)),
        ],
        out_specs=[
            pl.BlockSpec((1, T, QK_WIDTH), lambda b, i: (b, i, 0)),
            pl.BlockSpec((1, ATT_WIDTH, T), lambda b, i: (b, 0, i)),
            pl.BlockSpec((1, T, RWKV_IN), lambda b, i: (b, i, 0)),
        ],
        out_shape=[
            jax.ShapeDtypeStruct((B, S, QK_WIDTH), BF16),
            jax.ShapeDtypeStruct((B, ATT_WIDTH, S), BF16),
            jax.ShapeDtypeStruct((B, S, RWKV_IN), F32),
        ],
        compiler_params=_params(("parallel", "parallel")),
        name="in_proj",
    )(x, wqk, wvt, wrw)


def _attention_kernel(slopes_ref, q_ref, k_ref, vt_ref, lam_ref, g_ref, o_ref, bias_ref, s_ref, *, seq):
    h = pl.program_id(1)
    qi = pl.program_id(2)
    QB, KB = ATT_QB, ATT_KB
    slope = slopes_ref[h]

    q = q_ref[0]
    lane = lax.broadcasted_iota(jnp.int32, (QB, LANES), 1)
    zero = jnp.zeros_like(q)
    qq = jnp.concatenate([jnp.where(lane < ATT_QK_DIM, q, zero),
                          jnp.where(lane >= ATT_QK_DIM, q, zero)], axis=0)

    @pl.when(qi == 0)
    def _():
        for c in range(2 * seq // KB):
            r = lax.broadcasted_iota(jnp.int32, (KB, QB), 0) + (c * KB - seq)
            qq_ = lax.broadcasted_iota(jnp.int32, (KB, QB), 1)
            bias_ref[c * KB:(c + 1) * KB, :] = slope * jnp.abs(r - qq_).astype(F32)

    win = pl.multiple_of(seq - qi * QB, QB)
    nk = seq // KB
    m = jnp.full((1, 2 * QB), NEG_BIG, F32)
    for j in range(nk):
        kt = k_ref[0, j * KB:(j + 1) * KB, :]
        s = lax.dot_general(kt, qq, (((1,), (1,)), ((), ())), preferred_element_type=F32)
        bias = bias_ref[pl.ds(win + j * KB, KB), :]
        s = s - jnp.concatenate([bias, bias], axis=1)
        s_ref[j * KB:(j + 1) * KB, :] = s
        m = jnp.maximum(m, jnp.max(s, axis=0, keepdims=True))
    l = jnp.zeros((1, 2 * QB), F32)
    acc = jnp.zeros((ATT_V_DIM, 2 * QB), F32)
    for j in range(nk):
        p = jnp.exp2(s_ref[j * KB:(j + 1) * KB, :] - m)
        l = l + jnp.sum(p, axis=0, keepdims=True)
        vt = vt_ref[0, :, j * KB:(j + 1) * KB]
        acc = acc + jnp.dot(vt, p.astype(BF16), preferred_element_type=F32)

    lmb = lam_ref[...]
    lam = (jnp.exp(jnp.sum(lmb[0:1] * lmb[1:2], axis=1, keepdims=True))
           - jnp.exp(jnp.sum(lmb[2:3] * lmb[3:4], axis=1, keepdims=True)) + LAM_INIT)
    o = acc[:, :QB] / l[:, :QB] - lam * (acc[:, QB:] / l[:, QB:])
    ms = jnp.mean(o * o, axis=0, keepdims=True)
    o = o * lax.rsqrt(ms + RMS_EPS) * g_ref[...] * (1.0 - LAM_INIT)
    o_ref[0] = o.T.astype(BF16)


def _attention(qk, vt, att_lambda, subln_g):
    B, S, _ = qk.shape
    H = ATT_HEADS
    slopes = jnp.asarray([LOG2E * 2.0 ** (-8.0 * (i + 1) / H) for i in range(H)], F32)
    return pl.pallas_call(
        functools.partial(_attention_kernel, seq=S),
        grid_spec=pltpu.PrefetchScalarGridSpec(
            num_scalar_prefetch=1,
            grid=(B, H, S // ATT_QB),
            in_specs=[
                pl.BlockSpec((1, ATT_QB, LANES), lambda b, h, i, s: (b, i, h)),
                pl.BlockSpec((1, S, LANES), lambda b, h, i, s: (b, 0, H + h)),
                pl.BlockSpec((1, ATT_V_DIM, S), lambda b, h, i, s: (b, h, 0)),
                pl.BlockSpec((4, ATT_QK_DIM), lambda b, h, i, s: (0, 0)),
                pl.BlockSpec((ATT_V_DIM, 1), lambda b, h, i, s: (0, 0)),
            ],
            out_specs=pl.BlockSpec((1, ATT_QB, LANES), lambda b, h, i, s: (b, i, h)),
            scratch_shapes=[pltpu.VMEM((2 * S, ATT_QB), F32),
                            pltpu.VMEM((S, 2 * ATT_QB), F32)],
        ),
        out_shape=jax.ShapeDtypeStruct((B, S, ATT_WIDTH), BF16),
        compiler_params=_params(("parallel", "parallel", "arbitrary")),
        name="attention",
    )(slopes, qk, qk, vt, att_lambda, subln_g.reshape(ATT_V_DIM, 1))


def _head_sum(x, blk):
    return jnp.dot(x, blk, preferred_element_type=F32, precision=lax.Precision.HIGHEST)


def _rwkv_pre_kernel(p_ref, prev_ref, next_ref, mu_ref, w0_ref, wup_ref, a0_ref, aup_ref, gup_ref,
                     kk_ref, ka_ref, rk_ref, blk_ref,
                     r_ref, a_ref, lw_ref, kd_ref, kka_ref, vt_ref, g_ref, bonus_ref):
    i = pl.program_id(1)
    n = pl.num_programs(1)
    T = PRE_TILE
    W = RWKV_WIDTH
    p = p_ref[0]
    row = lax.broadcasted_iota(jnp.int32, p.shape, 0)
    prev_row = jnp.where(i > 0, prev_ref[0, 7:8, :], 0.0)
    next_row = jnp.where(i < n - 1, next_ref[0, 0:1, :], 0.0)
    prev = jnp.where(row == 0, prev_row, pltpu.roll(p, 1, axis=0))
    nxt = jnp.where(row == T - 1, next_row, pltpu.roll(p, T - 1, axis=0))
    p = p + mu_ref[0:1, :] * (prev - p) + mu_ref[1:2, :] * (nxt - p)

    r = p[:, 0:W]
    k = p[:, W:2 * W]
    v = p[:, 2 * W:3 * W]
    c = 3 * W
    wd = p[:, c:c + 2 * LORA_DECAY]
    ad = p[:, c + 2 * LORA_DECAY:c + 2 * LORA_DECAY + 2 * LORA_AAA]
    gd = p[:, c + 2 * LORA_DECAY + 2 * LORA_AAA:]

    wl = w0_ref[...] + _bdot(jnp.tanh(wd), wup_ref[...])
    lw = -math.exp(-0.5) * jax.nn.sigmoid(wl)
    av = jax.nn.sigmoid(a0_ref[...] + _bdot(ad, aup_ref[...]))
    g_ref[0] = _bdot(jax.nn.sigmoid(gd), gup_ref[...])

    blk = blk_ref[...]
    kkr = k * kk_ref[...]
    kk = kkr / jnp.maximum(jnp.sqrt(_head_sum(kkr * kkr, blk)), 1e-12)
    ksum = jnp.zeros_like(k)
    for d in range(2):
        a_d = av[:, d * W:(d + 1) * W]
        k_d = k * (1.0 + (a_d - 1.0) * ka_ref[...])
        ksum = ksum + k_d
        lw_ref[d, 0] = lw[:, d * W:(d + 1) * W]
        kd_ref[d, 0] = k_d
        kka_ref[d, 0] = kk * a_d
    r_ref[0] = r
    a_ref[0] = -kk
    vt_ref[0] = v.T
    bonus_ref[0] = _head_sum(r * ksum * rk_ref[...], blk) * v


def _rwkv_pre(rw, mu, w0cat, wup_blk, a0cat, aup_blk, gup, k_k, k_a, r_k, blk):
    B, S, _ = rw.shape
    T = PRE_TILE
    W = RWKV_WIDTH
    nb8 = S // 8
    full = lambda shape: pl.BlockSpec(shape, lambda b, i: (0,) * len(shape))
    row_spec = pl.BlockSpec((1, T, W), lambda b, i: (b, i, 0))
    dir_spec = pl.BlockSpec((2, 1, T, W), lambda b, i: (0, b, i, 0))
    row_shape = jax.ShapeDtypeStruct((B, S, W), F32)
    dir_shape = jax.ShapeDtypeStruct((2, B, S, W), F32)
    return pl.pallas_call(
        _rwkv_pre_kernel,
        grid=(B, S // T),
        in_specs=[
            pl.BlockSpec((1, T, RWKV_IN), lambda b, i: (b, i, 0)),
            pl.BlockSpec((1, 8, RWKV_IN), lambda b, i: (b, jnp.maximum(i * (T // 8) - 1, 0), 0)),
            pl.BlockSpec((1, 8, RWKV_IN), lambda b, i: (b, jnp.minimum((i + 1) * (T // 8), nb8 - 1), 0)),
            full((2, RWKV_IN)), full((1, 2 * W)), full((2 * LORA_DECAY, 2 * W)),
            full((1, 2 * W)), full((2 * LORA_AAA, 2 * W)), full((LORA_GATE, W)),
            full((1, W)), full((1, W)), full((1, W)), full((W, W)),
        ],
        out_specs=[row_spec, row_spec, dir_spec, dir_spec, dir_spec,
                   pl.BlockSpec((1, W, T), lambda b, i: (b, 0, i)), row_spec, row_spec],
        out_shape=[row_shape, row_shape, dir_shape, dir_shape, dir_shape,
                   jax.ShapeDtypeStruct((B, W, S), F32), row_shape, row_shape],
        compiler_params=_params(("parallel", "parallel")),
        name="rwkv_pre",
    )(rw, rw, rw, mu, w0cat, wup_blk, a0cat, aup_blk, gup, k_k, k_a, r_k, blk)


def _rwkv_scan_kernel(r_ref, a_ref, lw_ref, kd_ref, kka_ref, vt_ref, yt_ref, state_ref):
    d = pl.program_id(1)
    c = pl.program_id(2)
    C = CHUNK
    N = RWKV_HEAD_DIM

    @pl.when(c == 0)
    def _():
        state_ref[...] = jnp.zeros_like(state_ref)

    row = lax.broadcasted_iota(jnp.int32, (C, C), 0)
    col = lax.broadcasted_iota(jnp.int32, (C, C), 1)
    order = (row - col) * (1 - 2 * d)
    strict = order > 0
    incl = order >= 0
    tri = incl.astype(F32)
    eye = (row == col).astype(F32)

    lw = lw_ref[0, 0]
    k = kd_ref[0, 0]
    b = kka_ref[0, 0]
    cs = jnp.dot(tri, lw, preferred_element_type=F32, precision=lax.Precision.HIGHEST)
    tot = jnp.sum(lw, axis=0, keepdims=True)
    g_inv = jnp.exp(-cs)
    g_last = jnp.exp(tot - cs)
    g_tot = jnp.exp(tot)
    a_t = (a_ref[0] * jnp.exp(cs - lw)).astype(BF16)
    r_t = (r_ref[0] * jnp.exp(cs)).astype(BF16)
    lhs = jnp.concatenate([a_t, r_t], axis=0)
    rhs = jnp.concatenate([(b * g_inv).astype(BF16), (k * g_inv).astype(BF16)], axis=0)
    b_end = (b * g_last).astype(BF16)
    k_end = (k * g_last).astype(BF16)

    heads = range(RWKV_HEADS)
    sl = [slice(h * N, (h + 1) * N) for h in heads]
    vt = [vt_ref[0, sl[h], :].astype(BF16) for h in heads]
    s0 = [state_ref[h] for h in heads]
    gram = [_bdot_nt(lhs[:, sl[h]], rhs[:, sl[h]]) for h in heads]
    l_ab = [jnp.where(strict, gram[h][:C, :C], 0.0) for h in heads]
    l_ak = [jnp.where(strict, gram[h][:C, C:], 0.0).astype(BF16) for h in heads]
    m_rb = [jnp.where(incl, gram[h][C:, :C], 0.0).astype(BF16) for h in heads]
    m_rk = [jnp.where(incl, gram[h][C:, C:], 0.0).astype(BF16) for h in heads]

    t = [eye + l_ab[h] for h in heads]
    pw = l_ab
    for _ in range(int(math.log2(C)) - 1):
        pw = [_bdot(pw[h], pw[h]) for h in heads]
        t = [t[h] + _bdot(t[h], pw[h]) for h in heads]

    tb = [t[h].astype(BF16) for h in heads]
    w = [_bdot(tb[h], a_t[:, sl[h]]) for h in heads]
    x = [_bdot(tb[h], l_ak[h]) for h in heads]
    ut = [_bdot_nt(s0[h], w[h]) + _bdot_nt(vt[h], x[h]) for h in heads]
    for h in heads:
        yt_ref[0, 0, sl[h], :] = (_bdot_nt(s0[h], r_t[:, sl[h]]) + _bdot_nt(ut[h], m_rb[h])
                                  + _bdot_nt(vt[h], m_rk[h]))
    for h in heads:
        state_ref[h] = (s0[h] * g_tot[:, sl[h]] + _bdot(ut[h], b_end[:, sl[h]])
                        + _bdot(vt[h], k_end[:, sl[h]]))


def _rwkv_scan(r, a, lw, kd, kka, vt):
    B, S, W = r.shape
    C = CHUNK
    nc = S // C
    cidx = lambda d, c: c + d * (nc - 1 - 2 * c)
    row_spec = pl.BlockSpec((1, C, W), lambda b, d, c: (b, cidx(d, c), 0))
    dir_spec = pl.BlockSpec((1, 1, C, W), lambda b, d, c: (d, b, cidx(d, c), 0))
    return pl.pallas_call(
        _rwkv_scan_kernel,
        grid=(B, 2, nc),
        in_specs=[row_spec, row_spec, dir_spec, dir_spec, dir_spec,
                  pl.BlockSpec((1, W, C), lambda b, d, c: (b, 0, cidx(d, c)))],
        out_specs=pl.BlockSpec((1, 1, W, C), lambda b, d, c: (d, b, 0, cidx(d, c))),
        out_shape=jax.ShapeDtypeStruct((2, B, W, S), F32),
        scratch_shapes=[pltpu.VMEM((RWKV_HEADS, RWKV_HEAD_DIM, RWKV_HEAD_DIM), F32)],
        compiler_params=_params(("parallel", "parallel", "arbitrary")),
        name="rwkv_scan",
    )(r, a, lw, kd, kka, vt)


def _layernorm(z, g, b):
    mu = jnp.mean(z, axis=-1, keepdims=True)
    zc = z - mu
    var = jnp.mean(zc * zc, axis=-1, keepdims=True)
    return zc * lax.rsqrt(var + LN_EPS) * g + b


def _first_lane_where(mask, lane):
    return jnp.min(jnp.where(mask, lane, ROUTE_LANES), axis=1, keepdims=True)


def _mix_ln1_kernel(x_ref, att_ref, ytf_ref, ytb_ref, g_ref, bonus_ref, gng_ref, gnb_ref,
                    woa_ref, wob_ref, l1g_ref, l1b_ref, wr_ref, br_ref,
                    h_ref, exp_ref, gate_ref):
    T = MIX_TILE
    N = RWKV_HEAD_DIM
    yt = ytf_ref[0, 0] + ytb_ref[0, 0]
    y3 = yt.reshape(RWKV_HEADS, N, T)
    mu = jnp.mean(y3, axis=1, keepdims=True)
    yc = y3 - mu
    var = jnp.mean(yc * yc, axis=1, keepdims=True)
    yn = (yc * lax.rsqrt(var + RWKV_GN_EPS)).reshape(RWKV_WIDTH, T)
    yn = yn * gng_ref[...] + gnb_ref[...]
    tm = (yn.T + bonus_ref[0]) * g_ref[0]
    mix = (jnp.dot(att_ref[0], woa_ref[...], preferred_element_type=F32)
           + _bdot(tm, wob_ref[...]))
    h = _layernorm(ALPHA * x_ref[0] + mix, l1g_ref[...], l1b_ref[...])
    h_ref[0] = h

    logits = jnp.dot(h, wr_ref[...], preferred_element_type=F32,
                     precision=lax.Precision.HIGHEST) + br_ref[...]
    lane = lax.broadcasted_iota(jnp.int32, (T, ROUTE_LANES), 1)
    neg = jnp.asarray(-jnp.inf, F32)
    gl = jnp.where(lane < N_GROUPS, logits, neg)
    gmax = jnp.max(gl, axis=1, keepdims=True)
    grp = _first_lane_where(gl == gmax, lane)
    g1 = 1.0 / jnp.sum(jnp.exp(gl - gmax), axis=1, keepdims=True)
    lo = N_GROUPS + EXPERTS_PER_GROUP * grp
    sel = jnp.where((lane >= lo) & (lane < lo + EXPERTS_PER_GROUP), logits, neg)
    v1 = jnp.max(sel, axis=1, keepdims=True)
    i1 = _first_lane_where(sel == v1, lane)
    sel2 = jnp.where(lane == i1, neg, sel)
    v2 = jnp.max(sel2, axis=1, keepdims=True)
    i2 = _first_lane_where(sel2 == v2, lane)
    e2 = jnp.exp(v2 - v1)
    den = 1.0 + e2
    gate1 = g1 * (1.0 / den)
    gate2 = g1 * (e2 / den)
    exp_ref[0] = jnp.where(lane == 0, i1 - N_GROUPS, jnp.where(lane == 1, i2 - N_GROUPS, 0))
    gate_ref[0] = jnp.where(lane == 0, gate1, jnp.where(lane == 1, gate2, 0.0))


def _mix_ln1(x, att, yt, g, bonus, gn_g, gn_b, woa, wob, l1g, l1b, wr, br):
    B, S, D = x.shape
    T = MIX_TILE
    W = RWKV_WIDTH
    full = lambda shape: pl.BlockSpec(shape, lambda b, i: (0,) * len(shape))
    row = lambda width: pl.BlockSpec((1, T, width), lambda b, i: (b, i, 0))
    return pl.pallas_call(
        _mix_ln1_kernel,
        grid=(B, S // T),
        in_specs=[
            row(D), row(ATT_WIDTH),
            pl.BlockSpec((1, 1, W, T), lambda b, i: (0, b, 0, i)),
            pl.BlockSpec((1, 1, W, T), lambda b, i: (1, b, 0, i)),
            row(W), row(W), full((W, 1)), full((W, 1)),
            full((ATT_WIDTH, D)), full((W, D)), full((1, D)), full((1, D)),
            full((D, ROUTE_LANES)), full((1, ROUTE_LANES)),
        ],
        out_specs=[row(D), row(ROUTE_LANES), row(ROUTE_LANES)],
        out_shape=[jax.ShapeDtypeStruct((B, S, D), F32),
                   jax.ShapeDtypeStruct((B, S, ROUTE_LANES), jnp.int32),
                   jax.ShapeDtypeStruct((B, S, ROUTE_LANES), F32)],
        compiler_params=_params(("parallel", "parallel")),
        name="mix_ln1",
    )(x, att, yt, yt, g, bonus, gn_g, gn_b, woa, wob, l1g, l1b, wr, br)


def _start_row_gather(idx_ref, src_hbm, dst, sem, rows):
    def issue(r, carry):
        pltpu.make_async_copy(src_hbm.at[pl.ds(idx_ref[0, 0, r], 1), :],
                              dst.at[pl.ds(r, 1), :], sem).start()
        return carry
    lax.fori_loop(0, rows, issue, 0, unroll=8)


def _wait_row_gather(src_hbm, dst, sem, rows):
    pltpu.make_async_copy(src_hbm.at[pl.ds(0, rows), :], dst, sem).wait()


def _experts_kernel(bexp_ref, tok_ref, tok_next_ref, h_hbm, wg_ref, wu_ref, wd_ref, y_ref,
                    xbuf, sem):
    i = pl.program_id(0)
    n = pl.num_programs(0)
    R = MOE_BLOCK
    slot = i % 2

    @pl.when(i == 0)
    def _():
        _start_row_gather(tok_ref, h_hbm, xbuf.at[0], sem.at[0], R)

    @pl.when(i + 1 < n)
    def _():
        _start_row_gather(tok_next_ref, h_hbm, xbuf.at[1 - slot], sem.at[1 - slot], R)

    _wait_row_gather(h_hbm, xbuf.at[slot], sem.at[slot], R)
    xb = xbuf[slot].astype(BF16)
    hid = (jax.nn.silu(jnp.dot(xb, wg_ref[0], preferred_element_type=F32))
           * jnp.dot(xb, wu_ref[0], preferred_element_type=F32))
    y_ref[...] = _bdot(hid, wd_ref[0])


def _experts(h_flat, buf_tok, block_exp, wg, wu, wd):
    D = D_MODEL
    R = MOE_BLOCK
    nb = block_exp.shape[0]
    tok3 = buf_tok.reshape(nb, 1, R)
    idx_spec = lambda f: pl.BlockSpec((1, 1, R), f, memory_space=pltpu.SMEM)
    return pl.pallas_call(
        _experts_kernel,
        grid_spec=pltpu.PrefetchScalarGridSpec(
            num_scalar_prefetch=1,
            grid=(nb,),
            in_specs=[
                idx_spec(lambda i, e: (i, 0, 0)),
                idx_spec(lambda i, e: (jnp.minimum(i + 1, nb - 1), 0, 0)),
                pl.BlockSpec(memory_space=pl.ANY),
                pl.BlockSpec((1, D, EXPERT_HIDDEN), lambda i, e: (e[i], 0, 0)),
                pl.BlockSpec((1, D, EXPERT_HIDDEN), lambda i, e: (e[i], 0, 0)),
                pl.BlockSpec((1, EXPERT_HIDDEN, D), lambda i, e: (e[i], 0, 0)),
            ],
            out_specs=pl.BlockSpec((R, D), lambda i, e: (i, 0)),
            scratch_shapes=[pltpu.VMEM((2, R, D), F32), pltpu.SemaphoreType.DMA((2,))],
        ),
        out_shape=jax.ShapeDtypeStruct((nb * R, D), F32),
        compiler_params=_params(("arbitrary",)),
        name="experts",
    )(block_exp, tok3, tok3, h_flat, wg, wu, wd)


def _combine_ln2_kernel(pos_ref, pos_next_ref, h_ref, gate_ref, y_hbm, l2g_ref, l2b_ref, o_ref,
                        ybuf, sem):
    i = pl.program_id(0)
    n = pl.num_programs(0)
    T = OUT_TILE
    R = TOP_K * T
    slot = i % 2

    @pl.when(i == 0)
    def _():
        _start_row_gather(pos_ref, y_hbm, ybuf.at[0], sem.at[0], R)

    @pl.when(i + 1 < n)
    def _():
        _start_row_gather(pos_next_ref, y_hbm, ybuf.at[1 - slot], sem.at[1 - slot], R)

    _wait_row_gather(y_hbm, ybuf.at[slot], sem.at[slot], R)
    gates = gate_ref[...]
    moe = ybuf[slot, 0:T, :] * gates[:, 0:1] + ybuf[slot, T:R, :] * gates[:, 1:2]
    o_ref[...] = _layernorm(ALPHA * h_ref[...] + moe, l2g_ref[...], l2b_ref[...])


def _combine_ln2(h_flat, gates, pos, yb, l2g, l2b):
    n_tok, D = h_flat.shape
    T = OUT_TILE
    nt = n_tok // T
    idx_spec = lambda f: pl.BlockSpec((1, 1, TOP_K * T), f, memory_space=pltpu.SMEM)
    return pl.pallas_call(
        _combine_ln2_kernel,
        grid=(nt,),
        in_specs=[
            idx_spec(lambda i: (i, 0, 0)),
            idx_spec(lambda i: (jnp.minimum(i + 1, nt - 1), 0, 0)),
            pl.BlockSpec((T, D), lambda i: (i, 0)),
            pl.BlockSpec((T, ROUTE_LANES), lambda i: (i, 0)),
            pl.BlockSpec(memory_space=pl.ANY),
            pl.BlockSpec((1, D), lambda i: (0, 0)),
            pl.BlockSpec((1, D), lambda i: (0, 0)),
        ],
        out_specs=pl.BlockSpec((T, D), lambda i: (i, 0)),
        out_shape=jax.ShapeDtypeStruct((n_tok, D), F32),
        scratch_shapes=[pltpu.VMEM((2, TOP_K * T, D), F32), pltpu.SemaphoreType.DMA((2,))],
        compiler_params=_params(("arbitrary",)),
        name="combine_ln2",
    )(pos, pos, h_flat, gates, yb, l2g, l2b)


def _dispatch_plan(expert):
    n_tok = expert.shape[0]
    A = n_tok * TOP_K
    e_flat = expert.reshape(A)
    tok_flat = jnp.repeat(jnp.arange(n_tok, dtype=jnp.int32), TOP_K)
    order = jnp.argsort(e_flat)
    e_s, tok_s = e_flat[order], tok_flat[order]
    counts = jnp.zeros((N_EXPERTS,), jnp.int32).at[e_flat].add(1)
    padded = (counts + MOE_BLOCK - 1) // MOE_BLOCK * MOE_BLOCK
    starts = jnp.cumsum(counts) - counts
    pends = jnp.cumsum(padded)
    pstarts = pends - padded
    dest = pstarts[e_s] + jnp.arange(A, dtype=jnp.int32) - starts[e_s]
    nb = A // MOE_BLOCK + N_EXPERTS
    buf_tok = jnp.zeros((nb * MOE_BLOCK,), jnp.int32).at[dest].set(tok_s)
    block_start = jnp.arange(nb, dtype=jnp.int32) * MOE_BLOCK
    block_exp = jnp.minimum(jnp.sum(block_start[:, None] >= pends[None, :], axis=1),
                            N_EXPERTS - 1).astype(jnp.int32)
    pos = jnp.zeros((A,), jnp.int32).at[order].set(dest).reshape(n_tok, TOP_K)
    return buf_tok, block_exp, pos


def _trunk(x, w):
    B, S, D = x.shape
    qk, vt, rw = _in_proj(x, w["wqk"], w["wvt"], w["wrw"])
    att = _attention(qk, vt, w["att_lambda"], w["subln_g"])
    r, a, lw, kd, kka, rvt, g, bonus = _rwkv_pre(
        rw, w["mu"], w["w0cat"], w["wup_blk"], w["a0cat"], w["aup_blk"], w["gup"],
        w["k_k"], w["k_a"], w["r_k"], w["blk"])
    yt = _rwkv_scan(r, a, lw, kd, kka, rvt)
    h, expert, gates = _mix_ln1(x, att, yt, g, bonus, w["gn_g"], w["gn_b"], w["woa"], w["wob"],
                                w["l1g"], w["l1b"], w["wr"], w["br"])
    n_tok = B * S
    h_flat = h.reshape(n_tok, D)
    buf_tok, block_exp, pos = _dispatch_plan(expert.reshape(n_tok, ROUTE_LANES)[:, :TOP_K])
    yb = _experts(h_flat, buf_tok, block_exp, w["wg"], w["wu"], w["wd"])
    T = OUT_TILE
    pos_tiles = pos.reshape(n_tok // T, T, TOP_K).transpose(0, 2, 1).reshape(n_tok // T, 1, TOP_K * T)
    out = _combine_ln2(h_flat, gates.reshape(n_tok, ROUTE_LANES), pos_tiles, yb, w["l2g"], w["l2b"])
    return out.reshape(B, S, D)


def _block_diag2(m):
    z = jnp.zeros_like(m[0])
    return jnp.concatenate([jnp.concatenate([m[0], z], axis=1),
                            jnp.concatenate([z, m[1]], axis=1)], axis=0)


def kernel(x_prompt, x_sample, w_in, att_lambda, att_subln_g, rwkv_mu, rwkv_w0, rwkv_w_up, rwkv_a0, rwkv_a_up, rwkv_g_up, rwkv_k_k, rwkv_k_a, rwkv_r_k, rwkv_ln_g, rwkv_ln_b, w_out, ln1_g, ln1_b, router_g_w, router_g_b, router_e_w, router_e_b, exp_w_gate, exp_w_up, exp_w_down, ln2_g, ln2_b):
    assert w_in.shape[0] == DEPTH
    W = RWKV_WIDTH
    D = D_MODEL
    win = w_in[0]
    head = jnp.arange(W, dtype=jnp.int32) // RWKV_HEAD_DIM
    n_route = N_GROUPS + N_EXPERTS
    wr = jnp.concatenate([router_g_w[0],
                          jnp.transpose(router_e_w[0], (1, 0, 2)).reshape(D, N_EXPERTS)], axis=1)
    br = jnp.concatenate([router_g_b[0], router_e_b[0].reshape(N_EXPERTS)])
    w = {
        "wqk": win[:, :QK_WIDTH].astype(BF16),
        "wvt": win[:, QK_WIDTH:QK_WIDTH + ATT_WIDTH].T.astype(BF16),
        "wrw": win[:, QK_WIDTH + ATT_WIDTH:].astype(BF16),
        "att_lambda": att_lambda[0],
        "subln_g": att_subln_g[0],
        "mu": rwkv_mu[0],
        "w0cat": rwkv_w0[0].reshape(1, 2 * W),
        "wup_blk": _block_diag2(rwkv_w_up[0]).astype(BF16),
        "a0cat": rwkv_a0[0].reshape(1, 2 * W),
        "aup_blk": _block_diag2(rwkv_a_up[0]).astype(BF16),
        "gup": rwkv_g_up[0].astype(BF16),
        "k_k": rwkv_k_k[0].reshape(1, W),
        "k_a": rwkv_k_a[0].reshape(1, W),
        "r_k": rwkv_r_k[0].reshape(1, W),
        "blk": (head[:, None] == head[None, :]).astype(F32),
        "gn_g": rwkv_ln_g[0].reshape(W, 1),
        "gn_b": rwkv_ln_b[0].reshape(W, 1),
        "woa": w_out[0][:ATT_WIDTH].astype(BF16),
        "wob": w_out[0][ATT_WIDTH:].astype(BF16),
        "l1g": ln1_g[0].reshape(1, D),
        "l1b": ln1_b[0].reshape(1, D),
        "wr": jnp.pad(wr, ((0, 0), (0, ROUTE_LANES - n_route))),
        "br": jnp.pad(br, (0, ROUTE_LANES - n_route)).reshape(1, ROUTE_LANES),
        "wg": exp_w_gate[0].astype(BF16),
        "wu": exp_w_up[0].astype(BF16),
        "wd": exp_w_down[0].astype(BF16),
        "l2g": ln2_g[0].reshape(1, D),
        "l2b": ln2_b[0].reshape(1, D),
    }
    return (_trunk(x_prompt, w), _trunk(x_sample, w))
```

```python
import functools
import math

import jax
import jax.numpy as jnp
from jax import lax
from jax.experimental import pallas as pl
from jax.experimental.pallas import tpu as pltpu

F32 = jnp.float32
BF16 = jnp.bfloat16

D_MODEL = 1024
ATT_HEADS = 4
ATT_QK_DIM = 64
ATT_V_DIM = 128
ATT_WIDTH = ATT_HEADS * ATT_V_DIM
QK_WIDTH = 2 * ATT_HEADS * 2 * ATT_QK_DIM
RWKV_HEADS = 8
RWKV_HEAD_DIM = 64
RWKV_WIDTH = RWKV_HEADS * RWKV_HEAD_DIM
LORA_DECAY = 64
LORA_AAA = 64
LORA_GATE = 128
RWKV_IN = 3 * RWKV_WIDTH + 2 * LORA_DECAY + 2 * LORA_AAA + LORA_GATE
N_GROUPS = 4
EXPERTS_PER_GROUP = 8
N_EXPERTS = N_GROUPS * EXPERTS_PER_GROUP
TOP_K = 2
EXPERT_HIDDEN = 512
MOE_BLOCK = 128
DEPTH = 1
ALPHA = (2.0 * DEPTH) ** 0.25
LN_EPS = 1e-5
RMS_EPS = 1e-5
RWKV_GN_EPS = 64e-5
LAM_INIT = 0.8 - 0.6 * math.exp(-0.3 * 0)

LANES = 128
ROUTE_LANES = 128
VMEM_LIMIT = 56 * 1024 * 1024

PROJ_TILE = 512
ATT_QB = 128
ATT_KB = 512
PRE_TILE = 256
CHUNK = 128
MIX_TILE = 256
OUT_TILE = 256
NEG_BIG = -1e30
LOG2E = math.log2(math.e)
Q_SCALE = ATT_QK_DIM ** -0.5 * LOG2E


def _bdot(a, b):
    return jnp.dot(a.astype(BF16), b.astype(BF16), preferred_element_type=F32)


def _bdot_nt(a, b):
    return lax.dot_general(a.astype(BF16), b.astype(BF16), (((1,), (1,)), ((), ())),
                           preferred_element_type=F32)


def _split3(x):
    hi = x.astype(BF16)
    r1 = x - hi.astype(F32)
    mid = r1.astype(BF16)
    lo = (r1 - mid.astype(F32)).astype(BF16)
    return hi, mid, lo


def _params(sem):
    return pltpu.CompilerParams(dimension_semantics=sem, vmem_limit_bytes=VMEM_LIMIT)


def _in_proj_kernel(x_ref, wqk_ref, wvt_ref, wrw_ref, qk_ref, vt_ref, rw_ref):
    xb = x_ref[0].astype(BF16)
    qk = jnp.dot(xb, wqk_ref[...], preferred_element_type=F32)
    qk_ref[0, :, :QK_WIDTH // 2] = (qk[:, :QK_WIDTH // 2] * Q_SCALE).astype(BF16)
    qk_ref[0, :, QK_WIDTH // 2:] = qk[:, QK_WIDTH // 2:].astype(BF16)
    vt_ref[0] = lax.dot_general(wvt_ref[...], xb, (((1,), (1,)), ((), ())),
                                preferred_element_type=F32).astype(BF16)
    rw_ref[0] = jnp.dot(xb, wrw_ref[...], preferred_element_type=F32)


def _in_proj(x, wqk, wvt, wrw):
    B, S, D = x.shape
    T = PROJ_TILE
    return pl.pallas_call(
        _in_proj_kernel,
        grid=(B, S // T),
        in_specs=[
            pl.BlockSpec((1, T, D), lambda b, i: (b, i, 0)),
            pl.BlockSpec((D, QK_WIDTH), lambda b, i: (0, 0)),
            pl.BlockSpec((ATT_WIDTH, D), lambda b, i: (0, 0)),
            pl.BlockSpec((D, RWKV_IN), lambda b, i: (0, 0)),
        ],
        out_specs=[
            pl.BlockSpec((1, T, QK_WIDTH), lambda b, i: (b, i, 0)),
            pl.BlockSpec((1, ATT_WIDTH, T), lambda b, i: (b, 0, i)),
            pl.BlockSpec((1, T, RWKV_IN), lambda b, i: (b, i, 0)),
        ],
        out_shape=[
            jax.ShapeDtypeStruct((B, S, QK_WIDTH), BF16),
            jax.ShapeDtypeStruct((B, ATT_WIDTH, S), BF16),
            jax.ShapeDtypeStruct((B, S, RWKV_IN), F32),
        ],
        compiler_params=_params(("parallel", "parallel")),
        name="in_proj",
    )(x, wqk, wvt, wrw)


def _attention_kernel(slopes_ref, q_ref, k_ref, vt_ref, lam_ref, g_ref, o_ref, bias_ref, s_ref, *, seq):
    h = pl.program_id(1)
    qi = pl.program_id(2)
    QB, KB = ATT_QB, ATT_KB
    slope = slopes_ref[h]

    q = q_ref[0]
    lane = lax.broadcasted_iota(jnp.int32, (QB, LANES), 1)
    zero = jnp.zeros_like(q)
    qq = jnp.concatenate([jnp.where(lane < ATT_QK_DIM, q, zero),
                          jnp.where(lane >= ATT_QK_DIM, q, zero)], axis=0)

    @pl.when(qi == 0)
    def _():
        for c in range(2 * seq // KB):
            r = lax.broadcasted_iota(jnp.int32, (KB, QB), 0) + (c * KB - seq)
            qq_ = lax.broadcasted_iota(jnp.int32, (KB, QB), 1)
            bias_ref[c * KB:(c + 1) * KB, :] = slope * jnp.abs(r - qq_).astype(F32)

    win = pl.multiple_of(seq - qi * QB, QB)
    nk = seq // KB
    m = jnp.full((1, 2 * QB), NEG_BIG, F32)
    for j in range(nk):
        kt = k_ref[0, j * KB:(j + 1) * KB, :]
        s = lax.dot_general(kt, qq, (((1,), (1,)), ((), ())), preferred_element_type=F32)
        bias = bias_ref[pl.ds(win + j * KB, KB), :]
        s = s - jnp.concatenate([bias, bias], axis=1)
        s_ref[j * KB:(j + 1) * KB, :] = s
        m = jnp.maximum(m, jnp.max(s, axis=0, keepdims=True))
    l = jnp.zeros((1, 2 * QB), F32)
    acc = jnp.zeros((ATT_V_DIM, 2 * QB), F32)
    for j in range(nk):
        p = jnp.exp2(s_ref[j * KB:(j + 1) * KB, :] - m)
        l = l + jnp.sum(p, axis=0, keepdims=True)
        vt = vt_ref[0, :, j * KB:(j + 1) * KB]
        acc = acc + jnp.dot(vt, p.astype(BF16), preferred_element_type=F32)

    lmb = lam_ref[...]
    lam = (jnp.exp(jnp.sum(lmb[0:1] * lmb[1:2], axis=1, keepdims=True))
           - jnp.exp(jnp.sum(lmb[2:3] * lmb[3:4], axis=1, keepdims=True)) + LAM_INIT)
    o = acc[:, :QB] / l[:, :QB] - lam * (acc[:, QB:] / l[:, QB:])
    ms = jnp.mean(o * o, axis=0, keepdims=True)
    o = o * lax.rsqrt(ms + RMS_EPS) * g_ref[...] * (1.0 - LAM_INIT)
    o_ref[0] = o.T.astype(BF16)


def _attention(qk, vt, att_lambda, subln_g):
    B, S, _ = qk.shape
    H = ATT_HEADS
    slopes = jnp.asarray([LOG2E * 2.0 ** (-8.0 * (i + 1) / H) for i in range(H)], F32)
    return pl.pallas_call(
        functools.partial(_attention_kernel, seq=S),
        grid_spec=pltpu.PrefetchScalarGridSpec(
            num_scalar_prefetch=1,
            grid=(B, H, S // ATT_QB),
            in_specs=[
                pl.BlockSpec((1, ATT_QB, LANES), lambda b, h, i, s: (b, i, h)),
                pl.BlockSpec((1, S, LANES), lambda b, h, i, s: (b, 0, H + h)),
                pl.BlockSpec((1, ATT_V_DIM, S), lambda b, h, i, s: (b, h, 0)),
                pl.BlockSpec((4, ATT_QK_DIM), lambda b, h, i, s: (0, 0)),
                pl.BlockSpec((ATT_V_DIM, 1), lambda b, h, i, s: (0, 0)),
            ],
            out_specs=pl.BlockSpec((1, ATT_QB, LANES), lambda b, h, i, s: (b, i, h)),
            scratch_shapes=[pltpu.VMEM((2 * S, ATT_QB), F32),
                            pltpu.VMEM((S, 2 * ATT_QB), F32)],
        ),
        out_shape=jax.ShapeDtypeStruct((B, S, ATT_WIDTH), BF16),
        compiler_params=_params(("parallel", "parallel", "arbitrary")),
        name="attention",
    )(slopes, qk, qk, vt, att_lambda, subln_g.reshape(ATT_V_DIM, 1))


def _head_sum(x, blk):
    hi, mid, lo = _split3(x)
    return (jnp.dot(hi, blk, preferred_element_type=F32) + jnp.dot(mid, blk, preferred_element_type=F32)
            + jnp.dot(lo, blk, preferred_element_type=F32))


def _rwkv_pre_kernel(p_ref, prev_ref, next_ref, mu_ref, w0_ref, wup_ref, a0_ref, aup_ref, gup_ref,
                     kk_ref, ka_ref, rk_ref, blk_ref,
                     r_ref, a_ref, lw_ref, kd_ref, kka_ref, vt_ref, g_ref, bonus_ref):
    i = pl.program_id(1)
    n = pl.num_programs(1)
    T = PRE_TILE
    W = RWKV_WIDTH
    p = p_ref[0]
    row = lax.broadcasted_iota(jnp.int32, p.shape, 0)
    prev_row = jnp.where(i > 0, prev_ref[0, 7:8, :], 0.0)
    next_row = jnp.where(i < n - 1, next_ref[0, 0:1, :], 0.0)
    prev = jnp.where(row == 0, prev_row, pltpu.roll(p, 1, axis=0))
    nxt = jnp.where(row == T - 1, next_row, pltpu.roll(p, T - 1, axis=0))
    p = p + mu_ref[0:1, :] * (prev - p) + mu_ref[1:2, :] * (nxt - p)

    r = p[:, 0:W]
    k = p[:, W:2 * W]
    v = p[:, 2 * W:3 * W]
    c = 3 * W
    wd = p[:, c:c + 2 * LORA_DECAY]
    ad = p[:, c + 2 * LORA_DECAY:c + 2 * LORA_DECAY + 2 * LORA_AAA]
    gd = p[:, c + 2 * LORA_DECAY + 2 * LORA_AAA:]

    wl = w0_ref[...] + _bdot(jnp.tanh(wd), wup_ref[...])
    lw = -math.exp(-0.5) * jax.nn.sigmoid(wl)
    av = jax.nn.sigmoid(a0_ref[...] + _bdot(ad, aup_ref[...]))
    g_ref[0] = _bdot(jax.nn.sigmoid(gd), gup_ref[...])

    blk = blk_ref[...]
    kkr = k * kk_ref[...]
    kk = kkr / jnp.maximum(jnp.sqrt(_head_sum(kkr * kkr, blk)), 1e-12)
    ksum = jnp.zeros_like(k)
    for d in range(2):
        a_d = av[:, d * W:(d + 1) * W]
        k_d = k * (1.0 + (a_d - 1.0) * ka_ref[...])
        ksum = ksum + k_d
        lw_ref[d, 0] = lw[:, d * W:(d + 1) * W]
        kd_ref[d, 0] = k_d
        kka_ref[d, 0] = kk * a_d
    r_ref[0] = r
    a_ref[0] = -kk
    vt_ref[0] = v.T
    bonus_ref[0] = _head_sum(r * ksum * rk_ref[...], blk) * v


def _rwkv_pre(rw, mu, w0cat, wup_blk, a0cat, aup_blk, gup, k_k, k_a, r_k, blk):
    B, S, _ = rw.shape
    T = PRE_TILE
    W = RWKV_WIDTH
    nb8 = S // 8
    full = lambda shape: pl.BlockSpec(shape, lambda b, i: (0,) * len(shape))
    row_spec = pl.BlockSpec((1, T, W), lambda b, i: (b, i, 0))
    dir_spec = pl.BlockSpec((2, 1, T, W), lambda b, i: (0, b, i, 0))
    row_shape = jax.ShapeDtypeStruct((B, S, W), F32)
    dir_shape = jax.ShapeDtypeStruct((2, B, S, W), F32)
    return pl.pallas_call(
        _rwkv_pre_kernel,
        grid=(B, S // T),
        in_specs=[
            pl.BlockSpec((1, T, RWKV_IN), lambda b, i: (b, i, 0)),
            pl.BlockSpec((1, 8, RWKV_IN), lambda b, i: (b, jnp.maximum(i * (T // 8) - 1, 0), 0)),
            pl.BlockSpec((1, 8, RWKV_IN), lambda b, i: (b, jnp.minimum((i + 1) * (T // 8), nb8 - 1), 0)),
            full((2, RWKV_IN)), full((1, 2 * W)), full((2 * LORA_DECAY, 2 * W)),
            full((1, 2 * W)), full((2 * LORA_AAA, 2 * W)), full((LORA_GATE, W)),
            full((1, W)), full((1, W)), full((1, W)), full((W, W)),
        ],
        out_specs=[row_spec, row_spec, dir_spec, dir_spec, dir_spec,
                   pl.BlockSpec((1, W, T), lambda b, i: (b, 0, i)), row_spec, row_spec],
        out_shape=[row_shape, row_shape, dir_shape, dir_shape, dir_shape,
                   jax.ShapeDtypeStruct((B, W, S), F32), row_shape, row_shape],
        compiler_params=_params(("parallel", "parallel")),
        name="rwkv_pre",
    )(rw, rw, rw, mu, w0cat, wup_blk, a0cat, aup_blk, gup, k_k, k_a, r_k, blk)


def _rwkv_scan_kernel(r_ref, a_ref, lw_ref, kd_ref, kka_ref, vt_ref, yt_ref, state_ref):
    d = pl.program_id(1)
    c = pl.program_id(2)
    C = CHUNK
    N = RWKV_HEAD_DIM

    @pl.when(c == 0)
    def _():
        state_ref[...] = jnp.zeros_like(state_ref)

    row = lax.broadcasted_iota(jnp.int32, (C, C), 0)
    col = lax.broadcasted_iota(jnp.int32, (C, C), 1)
    order = (row - col) * (1 - 2 * d)
    strict = order > 0
    incl = order >= 0
    tri = jnp.where(incl, 1.0, 0.0).astype(BF16)
    eye = jnp.where(row == col, 1.0, 0.0)

    lw = lw_ref[0, 0]
    k = kd_ref[0, 0]
    b = kka_ref[0, 0]
    lw_hi, lw_mid, lw_lo = _split3(lw)
    cs = (jnp.dot(tri, lw_hi, preferred_element_type=F32) + jnp.dot(tri, lw_mid, preferred_element_type=F32)
          + jnp.dot(tri, lw_lo, preferred_element_type=F32))
    tot = jnp.sum(lw, axis=0, keepdims=True)
    g_inv = jnp.exp(-cs)
    g_last = jnp.exp(tot - cs)
    g_tot = jnp.exp(tot)
    a_t = (a_ref[0] * jnp.exp(cs - lw)).astype(BF16)
    r_t = (r_ref[0] * jnp.exp(cs)).astype(BF16)
    lhs = jnp.concatenate([a_t, r_t], axis=0)
    rhs = jnp.concatenate([(b * g_inv).astype(BF16), (k * g_inv).astype(BF16)], axis=0)
    b_end = (b * g_last).astype(BF16)
    k_end = (k * g_last).astype(BF16)

    heads = range(RWKV_HEADS)
    sl = [slice(h * N, (h + 1) * N) for h in heads]
    vt = [vt_ref[0, sl[h], :].astype(BF16) for h in heads]
    s0 = [state_ref[h] for h in heads]
    gram = [_bdot_nt(lhs[:, sl[h]], rhs[:, sl[h]]) for h in heads]
    l_ab = [jnp.where(strict, gram[h][:C, :C], 0.0) for h in heads]
    l_ak = [jnp.where(strict, gram[h][:C, C:], 0.0).astype(BF16) for h in heads]
    m_rb = [jnp.where(incl, gram[h][C:, :C], 0.0).astype(BF16) for h in heads]
    m_rk = [jnp.where(incl, gram[h][C:, C:], 0.0).astype(BF16) for h in heads]

    levels = int(math.log2(C))
    t = [eye + l_ab[h] for h in heads]
    pw = [_bdot(l_ab[h], l_ab[h]) for h in heads]
    for j in range(1, levels):
        pb = [pw[h].astype(BF16) for h in heads]
        if j < levels - 1:
            both = [_bdot(jnp.concatenate([pb[h], t[h].astype(BF16)], axis=0), pb[h]) for h in heads]
            pw = [both[h][:C] for h in heads]
            t = [t[h] + both[h][C:] for h in heads]
        else:
            t = [t[h] + _bdot(t[h], pb[h]) for h in heads]

    tb = [t[h].astype(BF16) for h in heads]
    xw = [_bdot(tb[h], jnp.concatenate([l_ak[h], a_t[:, sl[h]]], axis=1)) for h in heads]
    x = [xw[h][:, :C] for h in heads]
    w = [xw[h][:, C:] for h in heads]
    ut = [_bdot_nt(s0[h], w[h]) + _bdot_nt(vt[h], x[h]) for h in heads]
    for h in heads:
        yt_ref[0, 0, sl[h], :] = (_bdot_nt(s0[h], r_t[:, sl[h]]) + _bdot_nt(ut[h], m_rb[h])
                                  + _bdot_nt(vt[h], m_rk[h]))
    for h in heads:
        state_ref[h] = (s0[h] * g_tot[:, sl[h]] + _bdot(ut[h], b_end[:, sl[h]])
                        + _bdot(vt[h], k_end[:, sl[h]]))


def _rwkv_scan(r, a, lw, kd, kka, vt):
    B, S, W = r.shape
    C = CHUNK
    nc = S // C
    cidx = lambda d, c: c + d * (nc - 1 - 2 * c)
    row_spec = pl.BlockSpec((1, C, W), lambda b, d, c: (b, cidx(d, c), 0))
    dir_spec = pl.BlockSpec((1, 1, C, W), lambda b, d, c: (d, b, cidx(d, c), 0))
    return pl.pallas_call(
        _rwkv_scan_kernel,
        grid=(B, 2, nc),
        in_specs=[row_spec, row_spec, dir_spec, dir_spec, dir_spec,
                  pl.BlockSpec((1, W, C), lambda b, d, c: (b, 0, cidx(d, c)))],
        out_specs=pl.BlockSpec((1, 1, W, C), lambda b, d, c: (d, b, 0, cidx(d, c))),
        out_shape=jax.ShapeDtypeStruct((2, B, W, S), F32),
        scratch_shapes=[pltpu.VMEM((RWKV_HEADS, RWKV_HEAD_DIM, RWKV_HEAD_DIM), F32)],
        compiler_params=_params(("parallel", "parallel", "arbitrary")),
        name="rwkv_scan",
    )(r, a, lw, kd, kka, vt)


def _layernorm(z, g, b):
    mu = jnp.mean(z, axis=-1, keepdims=True)
    zc = z - mu
    var = jnp.mean(zc * zc, axis=-1, keepdims=True)
    return zc * lax.rsqrt(var + LN_EPS) * g + b


def _first_lane_where(mask, lane):
    return jnp.min(jnp.where(mask, lane, ROUTE_LANES), axis=1, keepdims=True)


def _mix_ln1_kernel(x_ref, att_ref, ytf_ref, ytb_ref, g_ref, bonus_ref, gng_ref, gnb_ref,
                    woa_ref, wob_ref, l1g_ref, l1b_ref, wrh_ref, wrl_ref, br_ref,
                    h_ref, exp_ref, gate_ref):
    T = MIX_TILE
    N = RWKV_HEAD_DIM
    yt = ytf_ref[0, 0] + ytb_ref[0, 0]
    y3 = yt.reshape(RWKV_HEADS, N, T)
    mu = jnp.mean(y3, axis=1, keepdims=True)
    yc = y3 - mu
    var = jnp.mean(yc * yc, axis=1, keepdims=True)
    yn = (yc * lax.rsqrt(var + RWKV_GN_EPS)).reshape(RWKV_WIDTH, T)
    yn = yn * gng_ref[...] + gnb_ref[...]
    tm = (yn.T + bonus_ref[0]) * g_ref[0]
    mix = (jnp.dot(att_ref[0], woa_ref[...], preferred_element_type=F32)
           + _bdot(tm, wob_ref[...]))
    h = _layernorm(ALPHA * x_ref[0] + mix, l1g_ref[...], l1b_ref[...])
    h_ref[0] = h

    h_hi = h.astype(BF16)
    h_lo = (h - h_hi.astype(F32)).astype(BF16)
    logits = (jnp.dot(h_hi, wrh_ref[...], preferred_element_type=F32)
              + jnp.dot(h_lo, wrh_ref[...], preferred_element_type=F32)
              + jnp.dot(h_hi, wrl_ref[...], preferred_element_type=F32)) + br_ref[...]
    lane = lax.broadcasted_iota(jnp.int32, (T, ROUTE_LANES), 1)
    neg = jnp.asarray(-jnp.inf, F32)
    gl = jnp.where(lane < N_GROUPS, logits, neg)
    gmax = jnp.max(gl, axis=1, keepdims=True)
    grp = _first_lane_where(gl == gmax, lane)
    g1 = 1.0 / jnp.sum(jnp.exp(gl - gmax), axis=1, keepdims=True)
    lo = N_GROUPS + EXPERTS_PER_GROUP * grp
    sel = jnp.where((lane >= lo) & (lane < lo + EXPERTS_PER_GROUP), logits, neg)
    v1 = jnp.max(sel, axis=1, keepdims=True)
    i1 = _first_lane_where(sel == v1, lane)
    sel2 = jnp.where(lane == i1, neg, sel)
    v2 = jnp.max(sel2, axis=1, keepdims=True)
    i2 = _first_lane_where(sel2 == v2, lane)
    e2 = jnp.exp(v2 - v1)
    den = 1.0 + e2
    gate1 = g1 * (1.0 / den)
    gate2 = g1 * (e2 / den)
    exp_ref[0] = jnp.where(lane == 0, i1 - N_GROUPS, jnp.where(lane == 1, i2 - N_GROUPS, 0))
    gate_ref[0] = jnp.where(lane == 0, gate1, jnp.where(lane == 1, gate2, 0.0))


def _mix_ln1(x, att, yt, g, bonus, gn_g, gn_b, woa, wob, l1g, l1b, wr_hi, wr_lo, br):
    B, S, D = x.shape
    T = MIX_TILE
    W = RWKV_WIDTH
    full = lambda shape: pl.BlockSpec(shape, lambda b, i: (0,) * len(shape))
    row = lambda width: pl.BlockSpec((1, T, width), lambda b, i: (b, i, 0))
    return pl.pallas_call(
        _mix_ln1_kernel,
        grid=(B, S // T),
        in_specs=[
            row(D), row(ATT_WIDTH),
            pl.BlockSpec((1, 1, W, T), lambda b, i: (0, b, 0, i)),
            pl.BlockSpec((1, 1, W, T), lambda b, i: (1, b, 0, i)),
            row(W), row(W), full((W, 1)), full((W, 1)),
            full((ATT_WIDTH, D)), full((W, D)), full((1, D)), full((1, D)),
            full((D, ROUTE_LANES)), full((D, ROUTE_LANES)), full((1, ROUTE_LANES)),
        ],
        out_specs=[row(D), row(ROUTE_LANES), row(ROUTE_LANES)],
        out_shape=[jax.ShapeDtypeStruct((B, S, D), F32),
                   jax.ShapeDtypeStruct((B, S, ROUTE_LANES), jnp.int32),
                   jax.ShapeDtypeStruct((B, S, ROUTE_LANES), F32)],
        compiler_params=_params(("parallel", "parallel")),
        name="mix_ln1",
    )(x, att, yt, yt, g, bonus, gn_g, gn_b, woa, wob, l1g, l1b, wr_hi, wr_lo, br)


def _start_row_gather(idx_ref, src_hbm, dst, sem, rows):
    def issue(r, carry):
        pltpu.make_async_copy(src_hbm.at[pl.ds(idx_ref[0, 0, r], 1), :],
                              dst.at[pl.ds(r, 1), :], sem).start()
        return carry
    lax.fori_loop(0, rows, issue, 0, unroll=8)


def _wait_row_gather(src_hbm, dst, sem, rows):
    pltpu.make_async_copy(src_hbm.at[pl.ds(0, rows), :], dst, sem).wait()


def _experts_kernel(bexp_ref, tok_ref, tok_next_ref, h_hbm, wg_ref, wu_ref, wd_ref, y_ref,
                    xbuf, sem):
    i = pl.program_id(0)
    n = pl.num_programs(0)
    R = MOE_BLOCK
    slot = i % 2

    @pl.when(i == 0)
    def _():
        _start_row_gather(tok_ref, h_hbm, xbuf.at[0], sem.at[0], R)

    @pl.when(i + 1 < n)
    def _():
        _start_row_gather(tok_next_ref, h_hbm, xbuf.at[1 - slot], sem.at[1 - slot], R)

    _wait_row_gather(h_hbm, xbuf.at[slot], sem.at[slot], R)
    xb = xbuf[slot].astype(BF16)
    hid = (jax.nn.silu(jnp.dot(xb, wg_ref[0], preferred_element_type=F32))
           * jnp.dot(xb, wu_ref[0], preferred_element_type=F32))
    y_ref[...] = _bdot(hid, wd_ref[0])


def _experts(h_flat, buf_tok, block_exp, wg, wu, wd):
    D = D_MODEL
    R = MOE_BLOCK
    nb = block_exp.shape[0]
    tok3 = buf_tok.reshape(nb, 1, R)
    idx_spec = lambda f: pl.BlockSpec((1, 1, R), f, memory_space=pltpu.SMEM)
    return pl.pallas_call(
        _experts_kernel,
        grid_spec=pltpu.PrefetchScalarGridSpec(
            num_scalar_prefetch=1,
            grid=(nb,),
            in_specs=[
                idx_spec(lambda i, e: (i, 0, 0)),
                idx_spec(lambda i, e: (jnp.minimum(i + 1, nb - 1), 0, 0)),
                pl.BlockSpec(memory_space=pl.ANY),
                pl.BlockSpec((1, D, EXPERT_HIDDEN), lambda i, e: (e[i], 0, 0)),
                pl.BlockSpec((1, D, EXPERT_HIDDEN), lambda i, e: (e[i], 0, 0)),
                pl.BlockSpec((1, EXPERT_HIDDEN, D), lambda i, e: (e[i], 0, 0)),
            ],
            out_specs=pl.BlockSpec((R, D), lambda i, e: (i, 0)),
            scratch_shapes=[pltpu.VMEM((2, R, D), F32), pltpu.SemaphoreType.DMA((2,))],
        ),
        out_shape=jax.ShapeDtypeStruct((nb * R, D), F32),
        compiler_params=_params(("arbitrary",)),
        name="experts",
    )(block_exp, tok3, tok3, h_flat, wg, wu, wd)


def _combine_ln2_kernel(pos_ref, pos_next_ref, h_ref, gate_ref, y_hbm, l2g_ref, l2b_ref, o_ref,
                        ybuf, sem):
    i = pl.program_id(0)
    n = pl.num_programs(0)
    T = OUT_TILE
    R = TOP_K * T
    slot = i % 2

    @pl.when(i == 0)
    def _():
        _start_row_gather(pos_ref, y_hbm, ybuf.at[0], sem.at[0], R)

    @pl.when(i + 1 < n)
    def _():
        _start_row_gather(pos_next_ref, y_hbm, ybuf.at[1 - slot], sem.at[1 - slot], R)

    _wait_row_gather(y_hbm, ybuf.at[slot], sem.at[slot], R)
    gates = gate_ref[...]
    moe = ybuf[slot, 0:T, :] * gates[:, 0:1] + ybuf[slot, T:R, :] * gates[:, 1:2]
    o_ref[...] = _layernorm(ALPHA * h_ref[...] + moe, l2g_ref[...], l2b_ref[...])


def _combine_ln2(h_flat, gates, pos, yb, l2g, l2b):
    n_tok, D = h_flat.shape
    T = OUT_TILE
    nt = n_tok // T
    idx_spec = lambda f: pl.BlockSpec((1, 1, TOP_K * T), f, memory_space=pltpu.SMEM)
    return pl.pallas_call(
        _combine_ln2_kernel,
        grid=(nt,),
        in_specs=[
            idx_spec(lambda i: (i, 0, 0)),
            idx_spec(lambda i: (jnp.minimum(i + 1, nt - 1), 0, 0)),
            pl.BlockSpec((T, D), lambda i: (i, 0)),
            pl.BlockSpec((T, ROUTE_LANES), lambda i: (i, 0)),
            pl.BlockSpec(memory_space=pl.ANY),
            pl.BlockSpec((1, D), lambda i: (0, 0)),
            pl.BlockSpec((1, D), lambda i: (0, 0)),
        ],
        out_specs=pl.BlockSpec((T, D), lambda i: (i, 0)),
        out_shape=jax.ShapeDtypeStruct((n_tok, D), F32),
        scratch_shapes=[pltpu.VMEM((2, TOP_K * T, D), F32), pltpu.SemaphoreType.DMA((2,))],
        compiler_params=_params(("arbitrary",)),
        name="combine_ln2",
    )(pos, pos, h_flat, gates, yb, l2g, l2b)


def _dispatch_plan(expert):
    n_tok = expert.shape[0]
    A = n_tok * TOP_K
    e_flat = expert.reshape(A)
    tok_flat = jnp.repeat(jnp.arange(n_tok, dtype=jnp.int32), TOP_K)
    order = jnp.argsort(e_flat)
    e_s, tok_s = e_flat[order], tok_flat[order]
    counts = jnp.zeros((N_EXPERTS,), jnp.int32).at[e_flat].add(1)
    padded = (counts + MOE_BLOCK - 1) // MOE_BLOCK * MOE_BLOCK
    starts = jnp.cumsum(counts) - counts
    pends = jnp.cumsum(padded)
    pstarts = pends - padded
    dest = pstarts[e_s] + jnp.arange(A, dtype=jnp.int32) - starts[e_s]
    nb = A // MOE_BLOCK + N_EXPERTS
    buf_tok = jnp.zeros((nb * MOE_BLOCK,), jnp.int32).at[dest].set(tok_s)
    block_start = jnp.arange(nb, dtype=jnp.int32) * MOE_BLOCK
    block_exp = jnp.minimum(jnp.sum(block_start[:, None] >= pends[None, :], axis=1),
                            N_EXPERTS - 1).astype(jnp.int32)
    pos = jnp.zeros((A,), jnp.int32).at[order].set(dest).reshape(n_tok, TOP_K)
    return buf_tok, block_exp, pos


def _trunk(x, w):
    B, S, D = x.shape
    qk, vt, rw = _in_proj(x, w["wqk"], w["wvt"], w["wrw"])
    att = _attention(qk, vt, w["att_lambda"], w["subln_g"])
    r, a, lw, kd, kka, rvt, g, bonus = _rwkv_pre(
        rw, w["mu"], w["w0cat"], w["wup_blk"], w["a0cat"], w["aup_blk"], w["gup"],
        w["k_k"], w["k_a"], w["r_k"], w["blk"])
    yt = _rwkv_scan(r, a, lw, kd, kka, rvt)
    h, expert, gates = _mix_ln1(x, att, yt, g, bonus, w["gn_g"], w["gn_b"], w["woa"], w["wob"],
                                w["l1g"], w["l1b"], w["wr_hi"], w["wr_lo"], w["br"])
    n_tok = B * S
    h_flat = h.reshape(n_tok, D)
    buf_tok, block_exp, pos = _dispatch_plan(expert.reshape(n_tok, ROUTE_LANES)[:, :TOP_K])
    yb = _experts(h_flat, buf_tok, block_exp, w["wg"], w["wu"], w["wd"])
    T = OUT_TILE
    pos_tiles = pos.reshape(n_tok // T, T, TOP_K).transpose(0, 2, 1).reshape(n_tok // T, 1, TOP_K * T)
    out = _combine_ln2(h_flat, gates.reshape(n_tok, ROUTE_LANES), pos_tiles, yb, w["l2g"], w["l2b"])
    return out.reshape(B, S, D)


def _block_diag2(m):
    z = jnp.zeros_like(m[0])
    return jnp.concatenate([jnp.concatenate([m[0], z], axis=1),
                            jnp.concatenate([z, m[1]], axis=1)], axis=0)


def kernel(x_prompt, x_sample, w_in, att_lambda, att_subln_g, rwkv_mu, rwkv_w0, rwkv_w_up, rwkv_a0, rwkv_a_up, rwkv_g_up, rwkv_k_k, rwkv_k_a, rwkv_r_k, rwkv_ln_g, rwkv_ln_b, w_out, ln1_g, ln1_b, router_g_w, router_g_b, router_e_w, router_e_b, exp_w_gate, exp_w_up, exp_w_down, ln2_g, ln2_b):
    assert w_in.shape[0] == DEPTH
    W = RWKV_WIDTH
    D = D_MODEL
    win = w_in[0]
    head = jnp.arange(W, dtype=jnp.int32) // RWKV_HEAD_DIM
    n_route = N_GROUPS + N_EXPERTS
    wr = jnp.concatenate([router_g_w[0],
                          jnp.transpose(router_e_w[0], (1, 0, 2)).reshape(D, N_EXPERTS)], axis=1)
    br = jnp.concatenate([router_g_b[0], router_e_b[0].reshape(N_EXPERTS)])
    wr_pad = jnp.pad(wr, ((0, 0), (0, ROUTE_LANES - n_route)))
    wr_hi = wr_pad.astype(BF16)
    w = {
        "wqk": win[:, :QK_WIDTH].astype(BF16),
        "wvt": win[:, QK_WIDTH:QK_WIDTH + ATT_WIDTH].T.astype(BF16),
        "wrw": win[:, QK_WIDTH + ATT_WIDTH:].astype(BF16),
        "att_lambda": att_lambda[0],
        "subln_g": att_subln_g[0],
        "mu": rwkv_mu[0],
        "w0cat": rwkv_w0[0].reshape(1, 2 * W),
        "wup_blk": _block_diag2(rwkv_w_up[0]).astype(BF16),
        "a0cat": rwkv_a0[0].reshape(1, 2 * W),
        "aup_blk": _block_diag2(rwkv_a_up[0]).astype(BF16),
        "gup": rwkv_g_up[0].astype(BF16),
        "k_k": rwkv_k_k[0].reshape(1, W),
        "k_a": rwkv_k_a[0].reshape(1, W),
        "r_k": rwkv_r_k[0].reshape(1, W),
        "blk": (head[:, None] == head[None, :]).astype(BF16),
        "gn_g": rwkv_ln_g[0].reshape(W, 1),
        "gn_b": rwkv_ln_b[0].reshape(W, 1),
        "woa": w_out[0][:ATT_WIDTH].astype(BF16),
        "wob": w_out[0][ATT_WIDTH:].astype(BF16),
        "l1g": ln1_g[0].reshape(1, D),
        "l1b": ln1_b[0].reshape(1, D),
        "wr_hi": wr_hi,
        "wr_lo": (wr_pad - wr_hi.astype(F32)).astype(BF16),
        "br": jnp.pad(br, (0, ROUTE_LANES - n_route)).reshape(1, ROUTE_LANES),
        "wg": exp_w_gate[0].astype(BF16),
        "wu": exp_w_up[0].astype(BF16),
        "wd": exp_w_down[0].astype(BF16),
        "l2g": ln2_g[0].reshape(1, D),
        "l2b": ln2_b[0].reshape(1, D),
    }
    return (_trunk(x_prompt, w), _trunk(x_sample, w))
```

```python
import functools
import math

import jax
import jax.numpy as jnp
from jax import lax
from jax.experimental import pallas as pl
from jax.experimental.pallas import tpu as pltpu

F32 = jnp.float32
BF16 = jnp.bfloat16

D_MODEL = 1024
ATT_HEADS = 4
ATT_QK_DIM = 64
ATT_V_DIM = 128
ATT_WIDTH = ATT_HEADS * ATT_V_DIM
QK_WIDTH = 2 * ATT_HEADS * 2 * ATT_QK_DIM
RWKV_HEADS = 8
RWKV_HEAD_DIM = 64
RWKV_WIDTH = RWKV_HEADS * RWKV_HEAD_DIM
LORA_DECAY = 64
LORA_AAA = 64
LORA_GATE = 128
RWKV_IN = 3 * RWKV_WIDTH + 2 * LORA_DECAY + 2 * LORA_AAA + LORA_GATE
N_GROUPS = 4
EXPERTS_PER_GROUP = 8
N_EXPERTS = N_GROUPS * EXPERTS_PER_GROUP
TOP_K = 2
EXPERT_HIDDEN = 512
MOE_BLOCK = 128
DEPTH = 1
ALPHA = (2.0 * DEPTH) ** 0.25
LN_EPS = 1e-5
RMS_EPS = 1e-5
RWKV_GN_EPS = 64e-5
LAM_INIT = 0.8 - 0.6 * math.exp(-0.3 * 0)

LANES = 128
ROUTE_LANES = 128
VMEM_LIMIT = 56 * 1024 * 1024

PROJ_TILE = 512
ATT_QB = 128
ATT_KB = 512
PRE_TILE = 256
CHUNK = 128
MIX_TILE = 256
OUT_TILE = 256
GATHER_DEPTH = 4
NEG_BIG = -1e30
LOG2E = math.log2(math.e)
Q_SCALE = ATT_QK_DIM ** -0.5 * LOG2E


def _bdot(a, b):
    return jnp.dot(a.astype(BF16), b.astype(BF16), preferred_element_type=F32)


def _bdot_nt(a, b):
    return lax.dot_general(a.astype(BF16), b.astype(BF16), (((1,), (1,)), ((), ())),
                           preferred_element_type=F32)


def _split3(x):
    hi = x.astype(BF16)
    r1 = x - hi.astype(F32)
    mid = r1.astype(BF16)
    lo = (r1 - mid.astype(F32)).astype(BF16)
    return hi, mid, lo


def _params(sem):
    return pltpu.CompilerParams(dimension_semantics=sem, vmem_limit_bytes=VMEM_LIMIT)


def _in_proj_kernel(x_ref, wqk_ref, wvt_ref, wrw_ref, qk_ref, vt_ref, rw_ref):
    xb = x_ref[0].astype(BF16)
    qk = jnp.dot(xb, wqk_ref[...], preferred_element_type=F32)
    qk_ref[0, :, :QK_WIDTH // 2] = (qk[:, :QK_WIDTH // 2] * Q_SCALE).astype(BF16)
    qk_ref[0, :, QK_WIDTH // 2:] = qk[:, QK_WIDTH // 2:].astype(BF16)
    vt_ref[0] = lax.dot_general(wvt_ref[...], xb, (((1,), (1,)), ((), ())),
                                preferred_element_type=F32).astype(BF16)
    rw_ref[0] = jnp.dot(xb, wrw_ref[...], preferred_element_type=F32)


def _in_proj(x, wqk, wvt, wrw):
    B, S, D = x.shape
    T = PROJ_TILE
    return pl.pallas_call(
        _in_proj_kernel,
        grid=(B, S // T),
        in_specs=[
            pl.BlockSpec((1, T, D), lambda b, i: (b, i, 0)),
            pl.BlockSpec((D, QK_WIDTH), lambda b, i: (0, 0)),
            pl.BlockSpec((ATT_WIDTH, D), lambda b, i: (0, 0)),
            pl.BlockSpec((D, RWKV_IN), lambda b, i: (0, 0)),
        ],
        out_specs=[
            pl.BlockSpec((1, T, QK_WIDTH), lambda b, i: (b, i, 0)),
            pl.BlockSpec((1, ATT_WIDTH, T), lambda b, i: (b, 0, i)),
            pl.BlockSpec((1, T, RWKV_IN), lambda b, i: (b, i, 0)),
        ],
        out_shape=[
            jax.ShapeDtypeStruct((B, S, QK_WIDTH), BF16),
            jax.ShapeDtypeStruct((B, ATT_WIDTH, S), BF16),
            jax.ShapeDtypeStruct((B, S, RWKV_IN), F32),
        ],
        compiler_params=_params(("parallel", "parallel")),
        name="in_proj",
    )(x, wqk, wvt, wrw)


def _stack_components(q):
    lane = lax.broadcasted_iota(jnp.int32, q.shape, 1)
    zero = jnp.zeros_like(q)
    return jnp.concatenate([jnp.where(lane < ATT_QK_DIM, q, zero),
                            jnp.where(lane >= ATT_QK_DIM, q, zero)], axis=0)


def _score_tile(k_ref, bias_ref, qq, win, j):
    KB = ATT_KB
    kt = k_ref[0, j * KB:(j + 1) * KB, :]
    s = lax.dot_general(kt, qq, (((1,), (1,)), ((), ())), preferred_element_type=F32)
    bias = bias_ref[pl.ds(win + j * KB, KB), :]
    return s - jnp.concatenate([bias, bias], axis=1)


def _scores_and_values(k_ref, vt_ref, bias_ref, qq_new, win_new, s_new, s_old, m_old, seq):
    QB, KB = ATT_QB, ATT_KB
    m_new = jnp.full((1, 2 * QB), NEG_BIG, F32)
    l = jnp.zeros((1, 2 * QB), F32)
    acc = jnp.zeros((ATT_V_DIM, 2 * QB), F32)
    for j in range(seq // KB):
        p = jnp.exp2(s_old[j * KB:(j + 1) * KB, :] - m_old)
        s = _score_tile(k_ref, bias_ref, qq_new, win_new, j)
        s_new[j * KB:(j + 1) * KB, :] = s
        m_new = jnp.maximum(m_new, jnp.max(s, axis=0, keepdims=True))
        l = l + jnp.sum(p, axis=0, keepdims=True)
        vt = vt_ref[0, :, j * KB:(j + 1) * KB]
        acc = acc + jnp.dot(vt, p.astype(BF16), preferred_element_type=F32)
    return m_new, l, acc


def _attention_kernel(slopes_ref, q_ref, qn_ref, k_ref, vt_ref, lam_ref, g_ref, o_ref,
                      bias_ref, sa_ref, sb_ref, ma_ref, *, seq):
    h = pl.program_id(1)
    i = pl.program_id(2)
    QB, KB = ATT_QB, ATT_KB
    nq = seq // QB
    slope = slopes_ref[h]

    @pl.when(i == 0)
    def _():
        for c in range(2 * seq // KB):
            r = lax.broadcasted_iota(jnp.int32, (KB, QB), 0) + (c * KB - seq)
            qq_ = lax.broadcasted_iota(jnp.int32, (KB, QB), 1)
            bias_ref[c * KB:(c + 1) * KB, :] = slope * jnp.abs(r - qq_).astype(F32)
        qq0 = _stack_components(q_ref[0, :QB, :])
        m0 = jnp.full((1, 2 * QB), NEG_BIG, F32)
        for j in range(seq // KB):
            s = _score_tile(k_ref, bias_ref, qq0, seq, j)
            sa_ref[j * KB:(j + 1) * KB, :] = s
            m0 = jnp.maximum(m0, jnp.max(s, axis=0, keepdims=True))
        ma_ref[...] = m0

    lmb = lam_ref[...]
    lam = (jnp.exp(jnp.sum(lmb[0:1] * lmb[1:2], axis=1, keepdims=True))
           - jnp.exp(jnp.sum(lmb[2:3] * lmb[3:4], axis=1, keepdims=True)) + LAM_INIT)

    def finish(l, acc):
        o = acc[:, :QB] / l[:, :QB] - lam * (acc[:, QB:] / l[:, QB:])
        ms = jnp.mean(o * o, axis=0, keepdims=True)
        o = o * lax.rsqrt(ms + RMS_EPS) * g_ref[...] * (1.0 - LAM_INIT)
        return o.T.astype(BF16)

    win1 = pl.multiple_of(seq - (2 * i + 1) * QB, QB)
    m_b, l, acc = _scores_and_values(k_ref, vt_ref, bias_ref, _stack_components(q_ref[0, QB:, :]), win1,
                                     sb_ref, sa_ref, ma_ref[...], seq)
    o_ref[0, :QB, :] = finish(l, acc)
    win2 = pl.multiple_of(seq - jnp.minimum(2 * i + 2, nq - 1) * QB, QB)
    m_a, l, acc = _scores_and_values(k_ref, vt_ref, bias_ref, _stack_components(qn_ref[0]), win2,
                                     sa_ref, sb_ref, m_b, seq)
    ma_ref[...] = m_a
    o_ref[0, QB:, :] = finish(l, acc)


def _attention(qk, vt, att_lambda, subln_g):
    B, S, _ = qk.shape
    H = ATT_HEADS
    nq = S // ATT_QB
    slopes = jnp.asarray([LOG2E * 2.0 ** (-8.0 * (i + 1) / H) for i in range(H)], F32)
    return pl.pallas_call(
        functools.partial(_attention_kernel, seq=S),
        grid_spec=pltpu.PrefetchScalarGridSpec(
            num_scalar_prefetch=1,
            grid=(B, H, nq // 2),
            in_specs=[
                pl.BlockSpec((1, 2 * ATT_QB, LANES), lambda b, h, i, s: (b, i, h)),
                pl.BlockSpec((1, ATT_QB, LANES),
                             lambda b, h, i, s: (b, jnp.minimum(2 * i + 2, nq - 1), h)),
                pl.BlockSpec((1, S, LANES), lambda b, h, i, s: (b, 0, H + h)),
                pl.BlockSpec((1, ATT_V_DIM, S), lambda b, h, i, s: (b, h, 0)),
                pl.BlockSpec((4, ATT_QK_DIM), lambda b, h, i, s: (0, 0)),
                pl.BlockSpec((ATT_V_DIM, 1), lambda b, h, i, s: (0, 0)),
            ],
            out_specs=pl.BlockSpec((1, 2 * ATT_QB, LANES), lambda b, h, i, s: (b, i, h)),
            scratch_shapes=[pltpu.VMEM((2 * S, ATT_QB), F32),
                            pltpu.VMEM((S, 2 * ATT_QB), F32),
                            pltpu.VMEM((S, 2 * ATT_QB), F32),
                            pltpu.VMEM((1, 2 * ATT_QB), F32)],
        ),
        out_shape=jax.ShapeDtypeStruct((B, S, ATT_WIDTH), BF16),
        compiler_params=_params(("parallel", "parallel", "arbitrary")),
        name="attention",
    )(slopes, qk, qk, qk, vt, att_lambda, subln_g.reshape(ATT_V_DIM, 1))


def _head_sum(x, blk):
    hi, mid, lo = _split3(x)
    return (jnp.dot(hi, blk, preferred_element_type=F32) + jnp.dot(mid, blk, preferred_element_type=F32)
            + jnp.dot(lo, blk, preferred_element_type=F32))


def _rwkv_pre_kernel(p_ref, prev_ref, next_ref, mu_ref, w0_ref, wup_ref, a0_ref, aup_ref, gup_ref,
                     kk_ref, ka_ref, rk_ref, blk_ref,
                     r_ref, a_ref, lw_ref, kd_ref, kka_ref, vt_ref, g_ref, bonus_ref):
    i = pl.program_id(1)
    n = pl.num_programs(1)
    T = PRE_TILE
    W = RWKV_WIDTH
    p = p_ref[0]
    row = lax.broadcasted_iota(jnp.int32, p.shape, 0)
    prev_row = jnp.where(i > 0, prev_ref[0, 7:8, :], 0.0)
    next_row = jnp.where(i < n - 1, next_ref[0, 0:1, :], 0.0)
    prev = jnp.where(row == 0, prev_row, pltpu.roll(p, 1, axis=0))
    nxt = jnp.where(row == T - 1, next_row, pltpu.roll(p, T - 1, axis=0))
    p = p + mu_ref[0:1, :] * (prev - p) + mu_ref[1:2, :] * (nxt - p)

    r = p[:, 0:W]
    k = p[:, W:2 * W]
    v = p[:, 2 * W:3 * W]
    c = 3 * W
    wd = p[:, c:c + 2 * LORA_DECAY]
    ad = p[:, c + 2 * LORA_DECAY:c + 2 * LORA_DECAY + 2 * LORA_AAA]
    gd = p[:, c + 2 * LORA_DECAY + 2 * LORA_AAA:]

    wl = w0_ref[...] + _bdot(jnp.tanh(wd), wup_ref[...])
    lw = -math.exp(-0.5) * jax.nn.sigmoid(wl)
    av = jax.nn.sigmoid(a0_ref[...] + _bdot(ad, aup_ref[...]))
    g_ref[0] = _bdot(jax.nn.sigmoid(gd), gup_ref[...])

    blk = blk_ref[...]
    kkr = k * kk_ref[...]
    kk = kkr / jnp.maximum(jnp.sqrt(_head_sum(kkr * kkr, blk)), 1e-12)
    ksum = jnp.zeros_like(k)
    for d in range(2):
        a_d = av[:, d * W:(d + 1) * W]
        k_d = k * (1.0 + (a_d - 1.0) * ka_ref[...])
        ksum = ksum + k_d
        lw_ref[d, 0] = lw[:, d * W:(d + 1) * W]
        kd_ref[d, 0] = k_d
        kka_ref[d, 0] = kk * a_d
    r_ref[0] = r
    a_ref[0] = -kk
    vt_ref[0] = v.T
    bonus_ref[0] = _head_sum(r * ksum * rk_ref[...], blk) * v


def _rwkv_pre(rw, mu, w0cat, wup_blk, a0cat, aup_blk, gup, k_k, k_a, r_k, blk):
    B, S, _ = rw.shape
    T = PRE_TILE
    W = RWKV_WIDTH
    nb8 = S // 8
    full = lambda shape: pl.BlockSpec(shape, lambda b, i: (0,) * len(shape))
    row_spec = pl.BlockSpec((1, T, W), lambda b, i: (b, i, 0))
    dir_spec = pl.BlockSpec((2, 1, T, W), lambda b, i: (0, b, i, 0))
    row_shape = jax.ShapeDtypeStruct((B, S, W), F32)
    dir_shape = jax.ShapeDtypeStruct((2, B, S, W), F32)
    return pl.pallas_call(
        _rwkv_pre_kernel,
        grid=(B, S // T),
        in_specs=[
            pl.BlockSpec((1, T, RWKV_IN), lambda b, i: (b, i, 0)),
            pl.BlockSpec((1, 8, RWKV_IN), lambda b, i: (b, jnp.maximum(i * (T // 8) - 1, 0), 0)),
            pl.BlockSpec((1, 8, RWKV_IN), lambda b, i: (b, jnp.minimum((i + 1) * (T // 8), nb8 - 1), 0)),
            full((2, RWKV_IN)), full((1, 2 * W)), full((2 * LORA_DECAY, 2 * W)),
            full((1, 2 * W)), full((2 * LORA_AAA, 2 * W)), full((LORA_GATE, W)),
            full((1, W)), full((1, W)), full((1, W)), full((W, W)),
        ],
        out_specs=[row_spec, row_spec, dir_spec, dir_spec, dir_spec,
                   pl.BlockSpec((1, W, T), lambda b, i: (b, 0, i)), row_spec, row_spec],
        out_shape=[row_shape, row_shape, dir_shape, dir_shape, dir_shape,
                   jax.ShapeDtypeStruct((B, W, S), F32), row_shape, row_shape],
        compiler_params=_params(("parallel", "parallel")),
        name="rwkv_pre",
    )(rw, rw, rw, mu, w0cat, wup_blk, a0cat, aup_blk, gup, k_k, k_a, r_k, blk)


def _rwkv_scan_kernel(r_ref, a_ref, lw_ref, kd_ref, kka_ref, vt_ref, yt_ref, state_ref):
    d = pl.program_id(1)
    c = pl.program_id(2)
    C = CHUNK
    N = RWKV_HEAD_DIM

    @pl.when(c == 0)
    def _():
        state_ref[...] = jnp.zeros_like(state_ref)

    row = lax.broadcasted_iota(jnp.int32, (C, C), 0)
    col = lax.broadcasted_iota(jnp.int32, (C, C), 1)
    order = (row - col) * (1 - 2 * d)
    strict = order > 0
    incl = order >= 0
    tri = jnp.where(incl, 1.0, 0.0).astype(BF16)
    eye = jnp.where(row == col, 1.0, 0.0)

    lw = lw_ref[0, 0]
    k = kd_ref[0, 0]
    b = kka_ref[0, 0]
    lw_hi, lw_mid, lw_lo = _split3(lw)
    cs = (jnp.dot(tri, lw_hi, preferred_element_type=F32) + jnp.dot(tri, lw_mid, preferred_element_type=F32)
          + jnp.dot(tri, lw_lo, preferred_element_type=F32))
    tot = jnp.sum(lw, axis=0, keepdims=True)
    g_inv = jnp.exp(-cs)
    g_last = jnp.exp(tot - cs)
    g_tot = jnp.exp(tot)
    a_t = (a_ref[0] * jnp.exp(cs - lw)).astype(BF16)
    r_t = (r_ref[0] * jnp.exp(cs)).astype(BF16)
    lhs = jnp.concatenate([a_t, r_t], axis=0)
    rhs = jnp.concatenate([(b * g_inv).astype(BF16), (k * g_inv).astype(BF16)], axis=0)
    b_end = (b * g_last).astype(BF16)
    k_end = (k * g_last).astype(BF16)

    heads = range(RWKV_HEADS)
    sl = [slice(h * N, (h + 1) * N) for h in heads]
    vt = [vt_ref[0, sl[h], :].astype(BF16) for h in heads]
    s0 = [state_ref[h] for h in heads]
    gram = [_bdot_nt(lhs[:, sl[h]], rhs[:, sl[h]]) for h in heads]
    l_ab = [jnp.where(strict, gram[h][:C, :C], 0.0) for h in heads]
    l_ak = [jnp.where(strict, gram[h][:C, C:], 0.0).astype(BF16) for h in heads]
    m_rb = [jnp.where(incl, gram[h][C:, :C], 0.0).astype(BF16) for h in heads]
    m_rk = [jnp.where(incl, gram[h][C:, C:], 0.0).astype(BF16) for h in heads]

    levels = int(math.log2(C))
    t = [eye + l_ab[h] for h in heads]
    pw = [_bdot(l_ab[h], l_ab[h]) for h in heads]
    for j in range(1, levels):
        pb = [pw[h].astype(BF16) for h in heads]
        if j < levels - 1:
            both = [_bdot(jnp.concatenate([pb[h], t[h].astype(BF16)], axis=0), pb[h]) for h in heads]
            pw = [both[h][:C] for h in heads]
            t = [t[h] + both[h][C:] for h in heads]
        else:
            t = [t[h] + _bdot(t[h], pb[h]) for h in heads]

    tb = [t[h].astype(BF16) for h in heads]
    xw = [_bdot(tb[h], jnp.concatenate([l_ak[h], a_t[:, sl[h]]], axis=1)) for h in heads]
    x = [xw[h][:, :C] for h in heads]
    w = [xw[h][:, C:] for h in heads]
    ut = [_bdot_nt(s0[h], w[h]) + _bdot_nt(vt[h], x[h]) for h in heads]
    for h in heads:
        yt_ref[0, 0, sl[h], :] = (_bdot_nt(s0[h], r_t[:, sl[h]]) + _bdot_nt(ut[h], m_rb[h])
                                  + _bdot_nt(vt[h], m_rk[h]))
    for h in heads:
        state_ref[h] = (s0[h] * g_tot[:, sl[h]] + _bdot(ut[h], b_end[:, sl[h]])
                        + _bdot(vt[h], k_end[:, sl[h]]))


def _rwkv_scan(r, a, lw, kd, kka, vt):
    B, S, W = r.shape
    C = CHUNK
    nc = S // C
    cidx = lambda d, c: c + d * (nc - 1 - 2 * c)
    row_spec = pl.BlockSpec((1, C, W), lambda b, d, c: (b, cidx(d, c), 0))
    dir_spec = pl.BlockSpec((1, 1, C, W), lambda b, d, c: (d, b, cidx(d, c), 0))
    return pl.pallas_call(
        _rwkv_scan_kernel,
        grid=(B, 2, nc),
        in_specs=[row_spec, row_spec, dir_spec, dir_spec, dir_spec,
                  pl.BlockSpec((1, W, C), lambda b, d, c: (b, 0, cidx(d, c)))],
        out_specs=pl.BlockSpec((1, 1, W, C), lambda b, d, c: (d, b, 0, cidx(d, c))),
        out_shape=jax.ShapeDtypeStruct((2, B, W, S), F32),
        scratch_shapes=[pltpu.VMEM((RWKV_HEADS, RWKV_HEAD_DIM, RWKV_HEAD_DIM), F32)],
        compiler_params=_params(("parallel", "parallel", "arbitrary")),
        name="rwkv_scan",
    )(r, a, lw, kd, kka, vt)


def _layernorm(z, g, b):
    mu = jnp.mean(z, axis=-1, keepdims=True)
    zc = z - mu
    var = jnp.mean(zc * zc, axis=-1, keepdims=True)
    return zc * lax.rsqrt(var + LN_EPS) * g + b


def _first_lane_where(mask, lane):
    return jnp.min(jnp.where(mask, lane, ROUTE_LANES), axis=1, keepdims=True)


def _mix_ln1_kernel(x_ref, att_ref, ytf_ref, ytb_ref, g_ref, bonus_ref, gng_ref, gnb_ref,
                    woa_ref, wob_ref, l1g_ref, l1b_ref, wrh_ref, wrl_ref, br_ref,
                    h_ref, exp_ref, gate_ref):
    T = MIX_TILE
    N = RWKV_HEAD_DIM
    yt = ytf_ref[0, 0] + ytb_ref[0, 0]
    y3 = yt.reshape(RWKV_HEADS, N, T)
    mu = jnp.mean(y3, axis=1, keepdims=True)
    yc = y3 - mu
    var = jnp.mean(yc * yc, axis=1, keepdims=True)
    yn = (yc * lax.rsqrt(var + RWKV_GN_EPS)).reshape(RWKV_WIDTH, T)
    yn = yn * gng_ref[...] + gnb_ref[...]
    tm = (yn.T + bonus_ref[0]) * g_ref[0]
    mix = (jnp.dot(att_ref[0], woa_ref[...], preferred_element_type=F32)
           + _bdot(tm, wob_ref[...]))
    h = _layernorm(ALPHA * x_ref[0] + mix, l1g_ref[...], l1b_ref[...])
    h_ref[0] = h

    h_hi = h.astype(BF16)
    h_lo = (h - h_hi.astype(F32)).astype(BF16)
    logits = (jnp.dot(h_hi, wrh_ref[...], preferred_element_type=F32)
              + jnp.dot(h_lo, wrh_ref[...], preferred_element_type=F32)
              + jnp.dot(h_hi, wrl_ref[...], preferred_element_type=F32)) + br_ref[...]
    lane = lax.broadcasted_iota(jnp.int32, (T, ROUTE_LANES), 1)
    neg = jnp.asarray(-jnp.inf, F32)
    gl = jnp.where(lane < N_GROUPS, logits, neg)
    gmax = jnp.max(gl, axis=1, keepdims=True)
    grp = _first_lane_where(gl == gmax, lane)
    g1 = 1.0 / jnp.sum(jnp.exp(gl - gmax), axis=1, keepdims=True)
    lo = N_GROUPS + EXPERTS_PER_GROUP * grp
    sel = jnp.where((lane >= lo) & (lane < lo + EXPERTS_PER_GROUP), logits, neg)
    v1 = jnp.max(sel, axis=1, keepdims=True)
    i1 = _first_lane_where(sel == v1, lane)
    sel2 = jnp.where(lane == i1, neg, sel)
    v2 = jnp.max(sel2, axis=1, keepdims=True)
    i2 = _first_lane_where(sel2 == v2, lane)
    e2 = jnp.exp(v2 - v1)
    den = 1.0 + e2
    gate1 = g1 * (1.0 / den)
    gate2 = g1 * (e2 / den)
    exp_ref[0] = jnp.where(lane == 0, i1 - N_GROUPS, jnp.where(lane == 1, i2 - N_GROUPS, 0))
    gate_ref[0] = jnp.where(lane == 0, gate1, jnp.where(lane == 1, gate2, 0.0))


def _mix_ln1(x, att, yt, g, bonus, gn_g, gn_b, woa, wob, l1g, l1b, wr_hi, wr_lo, br):
    B, S, D = x.shape
    T = MIX_TILE
    W = RWKV_WIDTH
    full = lambda shape: pl.BlockSpec(shape, lambda b, i: (0,) * len(shape))
    row = lambda width: pl.BlockSpec((1, T, width), lambda b, i: (b, i, 0))
    return pl.pallas_call(
        _mix_ln1_kernel,
        grid=(B, S // T),
        in_specs=[
            row(D), row(ATT_WIDTH),
            pl.BlockSpec((1, 1, W, T), lambda b, i: (0, b, 0, i)),
            pl.BlockSpec((1, 1, W, T), lambda b, i: (1, b, 0, i)),
            row(W), row(W), full((W, 1)), full((W, 1)),
            full((ATT_WIDTH, D)), full((W, D)), full((1, D)), full((1, D)),
            full((D, ROUTE_LANES)), full((D, ROUTE_LANES)), full((1, ROUTE_LANES)),
        ],
        out_specs=[row(D), row(ROUTE_LANES), row(ROUTE_LANES)],
        out_shape=[jax.ShapeDtypeStruct((B, S, D), F32),
                   jax.ShapeDtypeStruct((B, S, ROUTE_LANES), jnp.int32),
                   jax.ShapeDtypeStruct((B, S, ROUTE_LANES), F32)],
        compiler_params=_params(("parallel", "parallel")),
        name="mix_ln1",
    )(x, att, yt, yt, g, bonus, gn_g, gn_b, woa, wob, l1g, l1b, wr_hi, wr_lo, br)


def _start_row_gather(idx_ref, src_hbm, dst, sem, rows):
    def issue(r, carry):
        pltpu.make_async_copy(src_hbm.at[pl.ds(idx_ref[0, 0, r], 1), :],
                              dst.at[pl.ds(r, 1), :], sem).start()
        return carry
    lax.fori_loop(0, rows, issue, 0, unroll=8)


def _wait_row_gather(src_hbm, dst, sem, rows):
    pltpu.make_async_copy(src_hbm.at[pl.ds(0, rows), :], dst, sem).wait()


def _experts_kernel(bexp_ref, *refs):
    tok_refs = refs[:GATHER_DEPTH]
    h_hbm, wg_ref, wu_ref, wd_ref, y_ref, xbuf, sem = refs[GATHER_DEPTH:]
    i = pl.program_id(0)
    n = pl.num_programs(0)
    R = MOE_BLOCK
    slot = i % GATHER_DEPTH

    @pl.when(i == 0)
    def _():
        for a in range(GATHER_DEPTH - 1):
            _start_row_gather(tok_refs[a], h_hbm, xbuf.at[a], sem.at[a], R)

    ahead = i + (GATHER_DEPTH - 1)

    @pl.when(ahead < n)
    def _():
        aslot = ahead % GATHER_DEPTH
        _start_row_gather(tok_refs[GATHER_DEPTH - 1], h_hbm, xbuf.at[aslot], sem.at[aslot], R)

    _wait_row_gather(h_hbm, xbuf.at[slot], sem.at[slot], R)
    xb = xbuf[slot].astype(BF16)
    hid = (jax.nn.silu(jnp.dot(xb, wg_ref[0], preferred_element_type=F32))
           * jnp.dot(xb, wu_ref[0], preferred_element_type=F32))
    y_ref[...] = _bdot(hid, wd_ref[0])


def _experts(h_flat, buf_tok, block_exp, wg, wu, wd):
    D = D_MODEL
    R = MOE_BLOCK
    nb = block_exp.shape[0]
    assert nb >= GATHER_DEPTH
    tok3 = buf_tok.reshape(nb, 1, R)
    idx_spec = lambda a: pl.BlockSpec((1, 1, R), lambda i, e: (jnp.minimum(i + a, nb - 1), 0, 0),
                                      memory_space=pltpu.SMEM)
    return pl.pallas_call(
        _experts_kernel,
        grid_spec=pltpu.PrefetchScalarGridSpec(
            num_scalar_prefetch=1,
            grid=(nb,),
            in_specs=[idx_spec(a) for a in range(GATHER_DEPTH)] + [
                pl.BlockSpec(memory_space=pl.ANY),
                pl.BlockSpec((1, D, EXPERT_HIDDEN), lambda i, e: (e[i], 0, 0)),
                pl.BlockSpec((1, D, EXPERT_HIDDEN), lambda i, e: (e[i], 0, 0)),
                pl.BlockSpec((1, EXPERT_HIDDEN, D), lambda i, e: (e[i], 0, 0)),
            ],
            out_specs=pl.BlockSpec((R, D), lambda i, e: (i, 0)),
            scratch_shapes=[pltpu.VMEM((GATHER_DEPTH, R, D), F32),
                            pltpu.SemaphoreType.DMA((GATHER_DEPTH,))],
        ),
        out_shape=jax.ShapeDtypeStruct((nb * R, D), F32),
        compiler_params=_params(("arbitrary",)),
        name="experts",
    )(block_exp, *([tok3] * GATHER_DEPTH), h_flat, wg, wu, wd)


def _combine_ln2_kernel(pos_ref, pos_next_ref, h_ref, gate_ref, y_hbm, l2g_ref, l2b_ref, o_ref,
                        ybuf, sem):
    i = pl.program_id(0)
    n = pl.num_programs(0)
    T = OUT_TILE
    R = TOP_K * T
    slot = i % 2

    @pl.when(i == 0)
    def _():
        _start_row_gather(pos_ref, y_hbm, ybuf.at[0], sem.at[0], R)

    @pl.when(i + 1 < n)
    def _():
        _start_row_gather(pos_next_ref, y_hbm, ybuf.at[1 - slot], sem.at[1 - slot], R)

    _wait_row_gather(y_hbm, ybuf.at[slot], sem.at[slot], R)
    gates = gate_ref[...]
    moe = ybuf[slot, 0:T, :] * gates[:, 0:1] + ybuf[slot, T:R, :] * gates[:, 1:2]
    o_ref[...] = _layernorm(ALPHA * h_ref[...] + moe, l2g_ref[...], l2b_ref[...])


def _combine_ln2(h_flat, gates, pos, yb, l2g, l2b):
    n_tok, D = h_flat.shape
    T = OUT_TILE
    nt = n_tok // T
    idx_spec = lambda f: pl.BlockSpec((1, 1, TOP_K * T), f, memory_space=pltpu.SMEM)
    return pl.pallas_call(
        _combine_ln2_kernel,
        grid=(nt,),
        in_specs=[
            idx_spec(lambda i: (i, 0, 0)),
            idx_spec(lambda i: (jnp.minimum(i + 1, nt - 1), 0, 0)),
            pl.BlockSpec((T, D), lambda i: (i, 0)),
            pl.BlockSpec((T, ROUTE_LANES), lambda i: (i, 0)),
            pl.BlockSpec(memory_space=pl.ANY),
            pl.BlockSpec((1, D), lambda i: (0, 0)),
            pl.BlockSpec((1, D), lambda i: (0, 0)),
        ],
        out_specs=pl.BlockSpec((T, D), lambda i: (i, 0)),
        out_shape=jax.ShapeDtypeStruct((n_tok, D), F32),
        scratch_shapes=[pltpu.VMEM((2, TOP_K * T, D), F32), pltpu.SemaphoreType.DMA((2,))],
        compiler_params=_params(("arbitrary",)),
        name="combine_ln2",
    )(pos, pos, h_flat, gates, yb, l2g, l2b)


def _dispatch_plan(expert):
    n_tok = expert.shape[0]
    A = n_tok * TOP_K
    e_flat = expert.reshape(A)
    tok_flat = jnp.repeat(jnp.arange(n_tok, dtype=jnp.int32), TOP_K)
    order = jnp.argsort(e_flat)
    e_s, tok_s = e_flat[order], tok_flat[order]
    counts = jnp.zeros((N_EXPERTS,), jnp.int32).at[e_flat].add(1)
    padded = (counts + MOE_BLOCK - 1) // MOE_BLOCK * MOE_BLOCK
    starts = jnp.cumsum(counts) - counts
    pends = jnp.cumsum(padded)
    pstarts = pends - padded
    dest = pstarts[e_s] + jnp.arange(A, dtype=jnp.int32) - starts[e_s]
    nb = A // MOE_BLOCK + N_EXPERTS
    buf_tok = jnp.zeros((nb * MOE_BLOCK,), jnp.int32).at[dest].set(tok_s)
    block_start = jnp.arange(nb, dtype=jnp.int32) * MOE_BLOCK
    block_exp = jnp.minimum(jnp.sum(block_start[:, None] >= pends[None, :], axis=1),
                            N_EXPERTS - 1).astype(jnp.int32)
    pos = jnp.zeros((A,), jnp.int32).at[order].set(dest).reshape(n_tok, TOP_K)
    return buf_tok, block_exp, pos


def _trunk(x, w):
    B, S, D = x.shape
    qk, vt, rw = _in_proj(x, w["wqk"], w["wvt"], w["wrw"])
    att = _attention(qk, vt, w["att_lambda"], w["subln_g"])
    r, a, lw, kd, kka, rvt, g, bonus = _rwkv_pre(
        rw, w["mu"], w["w0cat"], w["wup_blk"], w["a0cat"], w["aup_blk"], w["gup"],
        w["k_k"], w["k_a"], w["r_k"], w["blk"])
    yt = _rwkv_scan(r, a, lw, kd, kka, rvt)
    h, expert, gates = _mix_ln1(x, att, yt, g, bonus, w["gn_g"], w["gn_b"], w["woa"], w["wob"],
                                w["l1g"], w["l1b"], w["wr_hi"], w["wr_lo"], w["br"])
    n_tok = B * S
    h_flat = h.reshape(n_tok, D)
    buf_tok, block_exp, pos = _dispatch_plan(expert.reshape(n_tok, ROUTE_LANES)[:, :TOP_K])
    yb = _experts(h_flat, buf_tok, block_exp, w["wg"], w["wu"], w["wd"])
    T = OUT_TILE
    pos_tiles = pos.reshape(n_tok // T, T, TOP_K).transpose(0, 2, 1).reshape(n_tok // T, 1, TOP_K * T)
    out = _combine_ln2(h_flat, gates.reshape(n_tok, ROUTE_LANES), pos_tiles, yb, w["l2g"], w["l2b"])
    return out.reshape(B, S, D)


def _block_diag2(m):
    z = jnp.zeros_like(m[0])
    return jnp.concatenate([jnp.concatenate([m[0], z], axis=1),
                            jnp.concatenate([z, m[1]], axis=1)], axis=0)


def kernel(x_prompt, x_sample, w_in, att_lambda, att_subln_g, rwkv_mu, rwkv_w0, rwkv_w_up, rwkv_a0, rwkv_a_up, rwkv_g_up, rwkv_k_k, rwkv_k_a, rwkv_r_k, rwkv_ln_g, rwkv_ln_b, w_out, ln1_g, ln1_b, router_g_w, router_g_b, router_e_w, router_e_b, exp_w_gate, exp_w_up, exp_w_down, ln2_g, ln2_b):
    assert w_in.shape[0] == DEPTH
    W = RWKV_WIDTH
    D = D_MODEL
    win = w_in[0]
    head = jnp.arange(W, dtype=jnp.int32) // RWKV_HEAD_DIM
    n_route = N_GROUPS + N_EXPERTS
    wr = jnp.concatenate([router_g_w[0],
                          jnp.transpose(router_e_w[0], (1, 0, 2)).reshape(D, N_EXPERTS)], axis=1)
    br = jnp.concatenate([router_g_b[0], router_e_b[0].reshape(N_EXPERTS)])
    wr_pad = jnp.pad(wr, ((0, 0), (0, ROUTE_LANES - n_route)))
    wr_hi = wr_pad.astype(BF16)
    w = {
        "wqk": win[:, :QK_WIDTH].astype(BF16),
        "wvt": win[:, QK_WIDTH:QK_WIDTH + ATT_WIDTH].T.astype(BF16),
        "wrw": win[:, QK_WIDTH + ATT_WIDTH:].astype(BF16),
        "att_lambda": att_lambda[0],
        "subln_g": att_subln_g[0],
        "mu": rwkv_mu[0],
        "w0cat": rwkv_w0[0].reshape(1, 2 * W),
        "wup_blk": _block_diag2(rwkv_w_up[0]).astype(BF16),
        "a0cat": rwkv_a0[0].reshape(1, 2 * W),
        "aup_blk": _block_diag2(rwkv_a_up[0]).astype(BF16),
        "gup": rwkv_g_up[0].astype(BF16),
        "k_k": rwkv_k_k[0].reshape(1, W),
        "k_a": rwkv_k_a[0].reshape(1, W),
        "r_k": rwkv_r_k[0].reshape(1, W),
        "blk": (head[:, None] == head[None, :]).astype(BF16),
        "gn_g": rwkv_ln_g[0].reshape(W, 1),
        "gn_b": rwkv_ln_b[0].reshape(W, 1),
        "woa": w_out[0][:ATT_WIDTH].astype(BF16),
        "wob": w_out[0][ATT_WIDTH:].astype(BF16),
        "l1g": ln1_g[0].reshape(1, D),
        "l1b": ln1_b[0].reshape(1, D),
        "wr_hi": wr_hi,
        "wr_lo": (wr_pad - wr_hi.astype(F32)).astype(BF16),
        "br": jnp.pad(br, (0, ROUTE_LANES - n_route)).reshape(1, ROUTE_LANES),
        "wg": exp_w_gate[0].astype(BF16),
        "wu": exp_w_up[0].astype(BF16),
        "wd": exp_w_down[0].astype(BF16),
        "l2g": ln2_g[0].reshape(1, D),
        "l2b": ln2_b[0].reshape(1, D),
    }
    return (_trunk(x_prompt, w), _trunk(x_sample, w))
```

```python
import functools
import math

import jax
import jax.numpy as jnp
from jax import lax
from jax.experimental import pallas as pl
from jax.experimental.pallas import tpu as pltpu

F32 = jnp.float32
BF16 = jnp.bfloat16

D_MODEL = 1024
ATT_HEADS = 4
ATT_QK_DIM = 64
ATT_V_DIM = 128
ATT_WIDTH = ATT_HEADS * ATT_V_DIM
QK_WIDTH = 2 * ATT_HEADS * 2 * ATT_QK_DIM
RWKV_HEADS = 8
RWKV_HEAD_DIM = 64
RWKV_WIDTH = RWKV_HEADS * RWKV_HEAD_DIM
LORA_DECAY = 64
LORA_AAA = 64
LORA_GATE = 128
RWKV_IN = 3 * RWKV_WIDTH + 2 * LORA_DECAY + 2 * LORA_AAA + LORA_GATE
N_GROUPS = 4
EXPERTS_PER_GROUP = 8
N_EXPERTS = N_GROUPS * EXPERTS_PER_GROUP
TOP_K = 2
EXPERT_HIDDEN = 512
MOE_BLOCK = 128
DEPTH = 1
ALPHA = (2.0 * DEPTH) ** 0.25
LN_EPS = 1e-5
RMS_EPS = 1e-5
RWKV_GN_EPS = 64e-5
LAM_INIT = 0.8 - 0.6 * math.exp(-0.3 * 0)

LANES = 128
ROUTE_LANES = 128
VMEM_LIMIT = 56 * 1024 * 1024

PROJ_TILE = 512
ATT_QB = 128
ATT_KB = 512
PRE_TILE = 256
CHUNK = 128
MIX_TILE = 256
OUT_TILE = 256
GATHER_DEPTH = 2
NEG_BIG = -1e30
LOG2E = math.log2(math.e)
Q_SCALE = ATT_QK_DIM ** -0.5 * LOG2E


def _bdot(a, b):
    return jnp.dot(a.astype(BF16), b.astype(BF16), preferred_element_type=F32)


def _bdot_nt(a, b):
    return lax.dot_general(a.astype(BF16), b.astype(BF16), (((1,), (1,)), ((), ())),
                           preferred_element_type=F32)


def _split3(x):
    hi = x.astype(BF16)
    r1 = x - hi.astype(F32)
    mid = r1.astype(BF16)
    lo = (r1 - mid.astype(F32)).astype(BF16)
    return hi, mid, lo


def _params(sem):
    return pltpu.CompilerParams(dimension_semantics=sem, vmem_limit_bytes=VMEM_LIMIT)


def _in_proj_kernel(x_ref, wqk_ref, wvt_ref, wrw_ref, qk_ref, vt_ref, rw_ref):
    xb = x_ref[0].astype(BF16)
    qk = jnp.dot(xb, wqk_ref[...], preferred_element_type=F32)
    qk_ref[0, :, :QK_WIDTH // 2] = (qk[:, :QK_WIDTH // 2] * Q_SCALE).astype(BF16)
    qk_ref[0, :, QK_WIDTH // 2:] = qk[:, QK_WIDTH // 2:].astype(BF16)
    vt_ref[0] = lax.dot_general(wvt_ref[...], xb, (((1,), (1,)), ((), ())),
                                preferred_element_type=F32).astype(BF16)
    rw_ref[0] = jnp.dot(xb, wrw_ref[...], preferred_element_type=F32)


def _in_proj(x, wqk, wvt, wrw):
    B, S, D = x.shape
    T = PROJ_TILE
    return pl.pallas_call(
        _in_proj_kernel,
        grid=(B, S // T),
        in_specs=[
            pl.BlockSpec((1, T, D), lambda b, i: (b, i, 0)),
            pl.BlockSpec((D, QK_WIDTH), lambda b, i: (0, 0)),
            pl.BlockSpec((ATT_WIDTH, D), lambda b, i: (0, 0)),
            pl.BlockSpec((D, RWKV_IN), lambda b, i: (0, 0)),
        ],
        out_specs=[
            pl.BlockSpec((1, T, QK_WIDTH), lambda b, i: (b, i, 0)),
            pl.BlockSpec((1, ATT_WIDTH, T), lambda b, i: (b, 0, i)),
            pl.BlockSpec((1, T, RWKV_IN), lambda b, i: (b, i, 0)),
        ],
        out_shape=[
            jax.ShapeDtypeStruct((B, S, QK_WIDTH), BF16),
            jax.ShapeDtypeStruct((B, ATT_WIDTH, S), BF16),
            jax.ShapeDtypeStruct((B, S, RWKV_IN), F32),
        ],
        compiler_params=_params(("parallel", "parallel")),
        name="in_proj",
    )(x, wqk, wvt, wrw)


def _stack_components(q):
    lane = lax.broadcasted_iota(jnp.int32, q.shape, 1)
    zero = jnp.zeros_like(q)
    return jnp.concatenate([jnp.where(lane < ATT_QK_DIM, q, zero),
                            jnp.where(lane >= ATT_QK_DIM, q, zero)], axis=0)


def _score_tile(k_ref, bias_ref, qq, win, j):
    KB = ATT_KB
    kt = k_ref[0, j * KB:(j + 1) * KB, :]
    s = lax.dot_general(kt, qq, (((1,), (1,)), ((), ())), preferred_element_type=F32)
    bias = bias_ref[pl.ds(win + j * KB, KB), :]
    return s - jnp.concatenate([bias, bias], axis=1)


def _scores_and_values(k_ref, vt_ref, bias_ref, qq_new, win_new, s_new, s_old, m_old, seq):
    QB, KB = ATT_QB, ATT_KB
    m_new = jnp.full((1, 2 * QB), NEG_BIG, F32)
    l = jnp.zeros((1, 2 * QB), F32)
    acc = jnp.zeros((ATT_V_DIM, 2 * QB), F32)
    for j in range(seq // KB):
        p = jnp.exp2(s_old[j * KB:(j + 1) * KB, :] - m_old)
        s = _score_tile(k_ref, bias_ref, qq_new, win_new, j)
        s_new[j * KB:(j + 1) * KB, :] = s
        m_new = jnp.maximum(m_new, jnp.max(s, axis=0, keepdims=True))
        l = l + jnp.sum(p, axis=0, keepdims=True)
        vt = vt_ref[0, :, j * KB:(j + 1) * KB]
        acc = acc + jnp.dot(vt, p.astype(BF16), preferred_element_type=F32)
    return m_new, l, acc


def _attention_kernel(slopes_ref, q_ref, qn_ref, k_ref, vt_ref, lam_ref, g_ref, o_ref,
                      bias_ref, sa_ref, sb_ref, ma_ref, *, seq):
    h = pl.program_id(1)
    i = pl.program_id(2)
    QB, KB = ATT_QB, ATT_KB
    nq = seq // QB
    slope = slopes_ref[h]

    @pl.when(i == 0)
    def _():
        for c in range(2 * seq // KB):
            r = lax.broadcasted_iota(jnp.int32, (KB, QB), 0) + (c * KB - seq)
            qq_ = lax.broadcasted_iota(jnp.int32, (KB, QB), 1)
            bias_ref[c * KB:(c + 1) * KB, :] = slope * jnp.abs(r - qq_).astype(F32)
        qq0 = _stack_components(q_ref[0, :QB, :])
        m0 = jnp.full((1, 2 * QB), NEG_BIG, F32)
        for j in range(seq // KB):
            s = _score_tile(k_ref, bias_ref, qq0, seq, j)
            sa_ref[j * KB:(j + 1) * KB, :] = s
            m0 = jnp.maximum(m0, jnp.max(s, axis=0, keepdims=True))
        ma_ref[...] = m0

    lmb = lam_ref[...]
    lam = (jnp.exp(jnp.sum(lmb[0:1] * lmb[1:2], axis=1, keepdims=True))
           - jnp.exp(jnp.sum(lmb[2:3] * lmb[3:4], axis=1, keepdims=True)) + LAM_INIT)

    def finish(l, acc):
        o = acc[:, :QB] / l[:, :QB] - lam * (acc[:, QB:] / l[:, QB:])
        ms = jnp.mean(o * o, axis=0, keepdims=True)
        o = o * lax.rsqrt(ms + RMS_EPS) * g_ref[...] * (1.0 - LAM_INIT)
        return o.T.astype(BF16)

    win1 = pl.multiple_of(seq - (2 * i + 1) * QB, QB)
    m_b, l, acc = _scores_and_values(k_ref, vt_ref, bias_ref, _stack_components(q_ref[0, QB:, :]), win1,
                                     sb_ref, sa_ref, ma_ref[...], seq)
    o_ref[0, :QB, :] = finish(l, acc)
    win2 = pl.multiple_of(seq - jnp.minimum(2 * i + 2, nq - 1) * QB, QB)
    m_a, l, acc = _scores_and_values(k_ref, vt_ref, bias_ref, _stack_components(qn_ref[0]), win2,
                                     sa_ref, sb_ref, m_b, seq)
    ma_ref[...] = m_a
    o_ref[0, QB:, :] = finish(l, acc)


def _attention(qk, vt, att_lambda, subln_g):
    B, S, _ = qk.shape
    H = ATT_HEADS
    nq = S // ATT_QB
    slopes = jnp.asarray([LOG2E * 2.0 ** (-8.0 * (i + 1) / H) for i in range(H)], F32)
    return pl.pallas_call(
        functools.partial(_attention_kernel, seq=S),
        grid_spec=pltpu.PrefetchScalarGridSpec(
            num_scalar_prefetch=1,
            grid=(B, H, nq // 2),
            in_specs=[
                pl.BlockSpec((1, 2 * ATT_QB, LANES), lambda b, h, i, s: (b, i, h)),
                pl.BlockSpec((1, ATT_QB, LANES),
                             lambda b, h, i, s: (b, jnp.minimum(2 * i + 2, nq - 1), h)),
                pl.BlockSpec((1, S, LANES), lambda b, h, i, s: (b, 0, H + h)),
                pl.BlockSpec((1, ATT_V_DIM, S), lambda b, h, i, s: (b, h, 0)),
                pl.BlockSpec((4, ATT_QK_DIM), lambda b, h, i, s: (0, 0)),
                pl.BlockSpec((ATT_V_DIM, 1), lambda b, h, i, s: (0, 0)),
            ],
            out_specs=pl.BlockSpec((1, 2 * ATT_QB, LANES), lambda b, h, i, s: (b, i, h)),
            scratch_shapes=[pltpu.VMEM((2 * S, ATT_QB), F32),
                            pltpu.VMEM((S, 2 * ATT_QB), F32),
                            pltpu.VMEM((S, 2 * ATT_QB), F32),
                            pltpu.VMEM((1, 2 * ATT_QB), F32)],
        ),
        out_shape=jax.ShapeDtypeStruct((B, S, ATT_WIDTH), BF16),
        compiler_params=_params(("parallel", "parallel", "arbitrary")),
        name="attention",
    )(slopes, qk, qk, qk, vt, att_lambda, subln_g.reshape(ATT_V_DIM, 1))


def _head_sum(x, blk):
    hi, mid, lo = _split3(x)
    return (jnp.dot(hi, blk, preferred_element_type=F32) + jnp.dot(mid, blk, preferred_element_type=F32)
            + jnp.dot(lo, blk, preferred_element_type=F32))


def _rwkv_pre_kernel(p_ref, prev_ref, next_ref, mu_ref, w0_ref, wup_ref, a0_ref, aup_ref, gup_ref,
                     kk_ref, ka_ref, rk_ref, blk_ref,
                     r_ref, a_ref, lw_ref, kd_ref, kka_ref, vt_ref, g_ref, bonus_ref):
    i = pl.program_id(1)
    n = pl.num_programs(1)
    T = PRE_TILE
    W = RWKV_WIDTH
    p = p_ref[0]
    row = lax.broadcasted_iota(jnp.int32, p.shape, 0)
    prev_row = jnp.where(i > 0, prev_ref[0, 7:8, :], 0.0)
    next_row = jnp.where(i < n - 1, next_ref[0, 0:1, :], 0.0)
    prev = jnp.where(row == 0, prev_row, pltpu.roll(p, 1, axis=0))
    nxt = jnp.where(row == T - 1, next_row, pltpu.roll(p, T - 1, axis=0))
    p = p + mu_ref[0:1, :] * (prev - p) + mu_ref[1:2, :] * (nxt - p)

    r = p[:, 0:W]
    k = p[:, W:2 * W]
    v = p[:, 2 * W:3 * W]
    c = 3 * W
    wd = p[:, c:c + 2 * LORA_DECAY]
    ad = p[:, c + 2 * LORA_DECAY:c + 2 * LORA_DECAY + 2 * LORA_AAA]
    gd = p[:, c + 2 * LORA_DECAY + 2 * LORA_AAA:]

    wl = w0_ref[...] + _bdot(jnp.tanh(wd), wup_ref[...])
    lw = -math.exp(-0.5) * jax.nn.sigmoid(wl)
    av = jax.nn.sigmoid(a0_ref[...] + _bdot(ad, aup_ref[...]))
    g_ref[0] = _bdot(jax.nn.sigmoid(gd), gup_ref[...])

    blk = blk_ref[...]
    kkr = k * kk_ref[...]
    kk = kkr / jnp.maximum(jnp.sqrt(_head_sum(kkr * kkr, blk)), 1e-12)
    ksum = jnp.zeros_like(k)
    for d in range(2):
        a_d = av[:, d * W:(d + 1) * W]
        k_d = k * (1.0 + (a_d - 1.0) * ka_ref[...])
        ksum = ksum + k_d
        lw_ref[d, 0] = lw[:, d * W:(d + 1) * W]
        kd_ref[d, 0] = k_d
        kka_ref[d, 0] = kk * a_d
    r_ref[0] = r
    a_ref[0] = -kk
    vt_ref[0] = v.T
    bonus_ref[0] = _head_sum(r * ksum * rk_ref[...], blk) * v


def _rwkv_pre(rw, mu, w0cat, wup_blk, a0cat, aup_blk, gup, k_k, k_a, r_k, blk):
    B, S, _ = rw.shape
    T = PRE_TILE
    W = RWKV_WIDTH
    nb8 = S // 8
    full = lambda shape: pl.BlockSpec(shape, lambda b, i: (0,) * len(shape))
    row_spec = pl.BlockSpec((1, T, W), lambda b, i: (b, i, 0))
    dir_spec = pl.BlockSpec((2, 1, T, W), lambda b, i: (0, b, i, 0))
    row_shape = jax.ShapeDtypeStruct((B, S, W), F32)
    dir_shape = jax.ShapeDtypeStruct((2, B, S, W), F32)
    return pl.pallas_call(
        _rwkv_pre_kernel,
        grid=(B, S // T),
        in_specs=[
            pl.BlockSpec((1, T, RWKV_IN), lambda b, i: (b, i, 0)),
            pl.BlockSpec((1, 8, RWKV_IN), lambda b, i: (b, jnp.maximum(i * (T // 8) - 1, 0), 0)),
            pl.BlockSpec((1, 8, RWKV_IN), lambda b, i: (b, jnp.minimum((i + 1) * (T // 8), nb8 - 1), 0)),
            full((2, RWKV_IN)), full((1, 2 * W)), full((2 * LORA_DECAY, 2 * W)),
            full((1, 2 * W)), full((2 * LORA_AAA, 2 * W)), full((LORA_GATE, W)),
            full((1, W)), full((1, W)), full((1, W)), full((W, W)),
        ],
        out_specs=[row_spec, row_spec, dir_spec, dir_spec, dir_spec,
                   pl.BlockSpec((1, W, T), lambda b, i: (b, 0, i)), row_spec, row_spec],
        out_shape=[row_shape, row_shape, dir_shape, dir_shape, dir_shape,
                   jax.ShapeDtypeStruct((B, W, S), F32), row_shape, row_shape],
        compiler_params=_params(("parallel", "parallel")),
        name="rwkv_pre",
    )(rw, rw, rw, mu, w0cat, wup_blk, a0cat, aup_blk, gup, k_k, k_a, r_k, blk)


def _rwkv_scan_kernel(r_ref, a_ref, lw_ref, kd_ref, kka_ref, vt_ref, yt_ref, state_ref):
    d = pl.program_id(1)
    c = pl.program_id(2)
    C = CHUNK
    N = RWKV_HEAD_DIM

    @pl.when(c == 0)
    def _():
        state_ref[...] = jnp.zeros_like(state_ref)

    row = lax.broadcasted_iota(jnp.int32, (C, C), 0)
    col = lax.broadcasted_iota(jnp.int32, (C, C), 1)
    order = (row - col) * (1 - 2 * d)
    strict = order > 0
    incl = order >= 0
    tri = jnp.where(incl, 1.0, 0.0).astype(BF16)
    eye = jnp.where(row == col, 1.0, 0.0)

    lw = lw_ref[0, 0]
    k = kd_ref[0, 0]
    b = kka_ref[0, 0]
    lw_hi, lw_mid, lw_lo = _split3(lw)
    cs = (jnp.dot(tri, lw_hi, preferred_element_type=F32) + jnp.dot(tri, lw_mid, preferred_element_type=F32)
          + jnp.dot(tri, lw_lo, preferred_element_type=F32))
    tot = jnp.sum(lw, axis=0, keepdims=True)
    g_inv = jnp.exp(-cs)
    g_last = jnp.exp(tot - cs)
    g_tot = jnp.exp(tot)
    a_t = (a_ref[0] * jnp.exp(cs - lw)).astype(BF16)
    r_t = (r_ref[0] * jnp.exp(cs)).astype(BF16)
    lhs = jnp.concatenate([a_t, r_t], axis=0)
    rhs = jnp.concatenate([(b * g_inv).astype(BF16), (k * g_inv).astype(BF16)], axis=0)
    b_end = (b * g_last).astype(BF16)
    k_end = (k * g_last).astype(BF16)

    heads = range(RWKV_HEADS)
    sl = [slice(h * N, (h + 1) * N) for h in heads]
    vt = [vt_ref[0, sl[h], :].astype(BF16) for h in heads]
    s0 = [state_ref[h] for h in heads]
    gram = [_bdot_nt(lhs[:, sl[h]], rhs[:, sl[h]]) for h in heads]
    l_ab = [jnp.where(strict, gram[h][:C, :C], 0.0) for h in heads]
    l_ak = [jnp.where(strict, gram[h][:C, C:], 0.0).astype(BF16) for h in heads]
    m_rb = [jnp.where(incl, gram[h][C:, :C], 0.0).astype(BF16) for h in heads]
    m_rk = [jnp.where(incl, gram[h][C:, C:], 0.0).astype(BF16) for h in heads]

    levels = int(math.log2(C))
    t = [eye + l_ab[h] for h in heads]
    pw = [_bdot(l_ab[h], l_ab[h]) for h in heads]
    for j in range(1, levels):
        pb = [pw[h].astype(BF16) for h in heads]
        if j < levels - 1:
            both = [_bdot(pb[h], jnp.concatenate([pb[h], t[h].astype(BF16)], axis=1)) for h in heads]
            pw = [both[h][:, :C] for h in heads]
            t = [t[h] + both[h][:, C:] for h in heads]
        else:
            t = [t[h] + _bdot(pb[h], t[h]) for h in heads]

    tb = [t[h].astype(BF16) for h in heads]
    xw = [_bdot(tb[h], jnp.concatenate([l_ak[h], a_t[:, sl[h]]], axis=1)) for h in heads]
    s0b = [s0[h].astype(BF16) for h in heads]
    ut = [_bdot_nt(jnp.concatenate([vt[h], s0b[h]], axis=1), xw[h]).astype(BF16)
          for h in heads]
    for h in heads:
        yt_ref[0, 0, sl[h], :] = _bdot_nt(
            jnp.concatenate([ut[h], vt[h], s0b[h]], axis=1),
            jnp.concatenate([m_rb[h], m_rk[h], r_t[:, sl[h]]], axis=1))
    for h in heads:
        state_ref[h] = s0[h] * g_tot[:, sl[h]] + _bdot(
            jnp.concatenate([ut[h], vt[h]], axis=1),
            jnp.concatenate([b_end[:, sl[h]], k_end[:, sl[h]]], axis=0))


def _rwkv_scan(r, a, lw, kd, kka, vt):
    B, S, W = r.shape
    C = CHUNK
    nc = S // C
    cidx = lambda d, c: c + d * (nc - 1 - 2 * c)
    row_spec = pl.BlockSpec((1, C, W), lambda b, d, c: (b, cidx(d, c), 0))
    dir_spec = pl.BlockSpec((1, 1, C, W), lambda b, d, c: (d, b, cidx(d, c), 0))
    return pl.pallas_call(
        _rwkv_scan_kernel,
        grid=(B, 2, nc),
        in_specs=[row_spec, row_spec, dir_spec, dir_spec, dir_spec,
                  pl.BlockSpec((1, W, C), lambda b, d, c: (b, 0, cidx(d, c)))],
        out_specs=pl.BlockSpec((1, 1, W, C), lambda b, d, c: (d, b, 0, cidx(d, c))),
        out_shape=jax.ShapeDtypeStruct((2, B, W, S), F32),
        scratch_shapes=[pltpu.VMEM((RWKV_HEADS, RWKV_HEAD_DIM, RWKV_HEAD_DIM), F32)],
        compiler_params=_params(("parallel", "parallel", "arbitrary")),
        name="rwkv_scan",
    )(r, a, lw, kd, kka, vt)


def _layernorm(z, g, b):
    mu = jnp.mean(z, axis=-1, keepdims=True)
    zc = z - mu
    var = jnp.mean(zc * zc, axis=-1, keepdims=True)
    return zc * lax.rsqrt(var + LN_EPS) * g + b


def _first_lane_where(mask, lane):
    return jnp.min(jnp.where(mask, lane, ROUTE_LANES), axis=1, keepdims=True)


def _mix_ln1_kernel(x_ref, att_ref, ytf_ref, ytb_ref, g_ref, bonus_ref, gng_ref, gnb_ref,
                    woa_ref, wob_ref, l1g_ref, l1b_ref, wrh_ref, wrl_ref, br_ref,
                    h_ref, exp_ref, gate_ref):
    T = MIX_TILE
    N = RWKV_HEAD_DIM
    yt = ytf_ref[0, 0] + ytb_ref[0, 0]
    y3 = yt.reshape(RWKV_HEADS, N, T)
    mu = jnp.mean(y3, axis=1, keepdims=True)
    yc = y3 - mu
    var = jnp.mean(yc * yc, axis=1, keepdims=True)
    yn = (yc * lax.rsqrt(var + RWKV_GN_EPS)).reshape(RWKV_WIDTH, T)
    yn = yn * gng_ref[...] + gnb_ref[...]
    tm = (yn.T + bonus_ref[0]) * g_ref[0]
    mix = (jnp.dot(att_ref[0], woa_ref[...], preferred_element_type=F32)
           + _bdot(tm, wob_ref[...]))
    h = _layernorm(ALPHA * x_ref[0] + mix, l1g_ref[...], l1b_ref[...])
    h_ref[0] = h

    h_hi = h.astype(BF16)
    h_lo = (h - h_hi.astype(F32)).astype(BF16)
    logits = (jnp.dot(h_hi, wrh_ref[...], preferred_element_type=F32)
              + jnp.dot(h_lo, wrh_ref[...], preferred_element_type=F32)
              + jnp.dot(h_hi, wrl_ref[...], preferred_element_type=F32)) + br_ref[...]
    lane = lax.broadcasted_iota(jnp.int32, (T, ROUTE_LANES), 1)
    neg = jnp.asarray(-jnp.inf, F32)
    gl = jnp.where(lane < N_GROUPS, logits, neg)
    gmax = jnp.max(gl, axis=1, keepdims=True)
    grp = _first_lane_where(gl == gmax, lane)
    g1 = 1.0 / jnp.sum(jnp.exp(gl - gmax), axis=1, keepdims=True)
    lo = N_GROUPS + EXPERTS_PER_GROUP * grp
    sel = jnp.where((lane >= lo) & (lane < lo + EXPERTS_PER_GROUP), logits, neg)
    v1 = jnp.max(sel, axis=1, keepdims=True)
    i1 = _first_lane_where(sel == v1, lane)
    sel2 = jnp.where(lane == i1, neg, sel)
    v2 = jnp.max(sel2, axis=1, keepdims=True)
    i2 = _first_lane_where(sel2 == v2, lane)
    e2 = jnp.exp(v2 - v1)
    den = 1.0 + e2
    gate1 = g1 * (1.0 / den)
    gate2 = g1 * (e2 / den)
    exp_ref[0] = jnp.where(lane == 0, i1 - N_GROUPS, jnp.where(lane == 1, i2 - N_GROUPS, 0))
    gate_ref[0] = jnp.where(lane == 0, gate1, jnp.where(lane == 1, gate2, 0.0))


def _mix_ln1(x, att, yt, g, bonus, gn_g, gn_b, woa, wob, l1g, l1b, wr_hi, wr_lo, br):
    B, S, D = x.shape
    T = MIX_TILE
    W = RWKV_WIDTH
    full = lambda shape: pl.BlockSpec(shape, lambda b, i: (0,) * len(shape))
    row = lambda width: pl.BlockSpec((1, T, width), lambda b, i: (b, i, 0))
    return pl.pallas_call(
        _mix_ln1_kernel,
        grid=(B, S // T),
        in_specs=[
            row(D), row(ATT_WIDTH),
            pl.BlockSpec((1, 1, W, T), lambda b, i: (0, b, 0, i)),
            pl.BlockSpec((1, 1, W, T), lambda b, i: (1, b, 0, i)),
            row(W), row(W), full((W, 1)), full((W, 1)),
            full((ATT_WIDTH, D)), full((W, D)), full((1, D)), full((1, D)),
            full((D, ROUTE_LANES)), full((D, ROUTE_LANES)), full((1, ROUTE_LANES)),
        ],
        out_specs=[row(D), row(ROUTE_LANES), row(ROUTE_LANES)],
        out_shape=[jax.ShapeDtypeStruct((B, S, D), F32),
                   jax.ShapeDtypeStruct((B, S, ROUTE_LANES), jnp.int32),
                   jax.ShapeDtypeStruct((B, S, ROUTE_LANES), F32)],
        compiler_params=_params(("parallel", "parallel")),
        name="mix_ln1",
    )(x, att, yt, yt, g, bonus, gn_g, gn_b, woa, wob, l1g, l1b, wr_hi, wr_lo, br)


def _start_row_gather(idx_ref, src_hbm, dst, sem, rows):
    for r in range(rows):
        pltpu.make_async_copy(src_hbm.at[pl.ds(idx_ref[0, 0, r], 1), :],
                              dst.at[pl.ds(r, 1), :], sem).start()


def _wait_row_gather(src_hbm, dst, sem, rows):
    pltpu.make_async_copy(src_hbm.at[pl.ds(0, rows), :], dst, sem).wait()


def _experts_kernel(bexp_ref, *refs):
    tok_refs = refs[:GATHER_DEPTH]
    h_hbm, wg_ref, wu_ref, wd_ref, y_ref, xbuf, sem = refs[GATHER_DEPTH:]
    i = pl.program_id(0)
    n = pl.num_programs(0)
    R = MOE_BLOCK
    slot = i % GATHER_DEPTH

    @pl.when(i == 0)
    def _():
        for a in range(GATHER_DEPTH - 1):
            _start_row_gather(tok_refs[a], h_hbm, xbuf.at[a], sem.at[a], R)

    ahead = i + (GATHER_DEPTH - 1)

    @pl.when(ahead < n)
    def _():
        aslot = ahead % GATHER_DEPTH
        _start_row_gather(tok_refs[GATHER_DEPTH - 1], h_hbm, xbuf.at[aslot], sem.at[aslot], R)

    _wait_row_gather(h_hbm, xbuf.at[slot], sem.at[slot], R)
    xb = xbuf[slot].astype(BF16)
    hid = (jax.nn.silu(jnp.dot(xb, wg_ref[0], preferred_element_type=F32))
           * jnp.dot(xb, wu_ref[0], preferred_element_type=F32))
    y_ref[...] = _bdot(hid, wd_ref[0])


def _experts(h_flat, buf_tok, block_exp, wg, wu, wd):
    D = D_MODEL
    R = MOE_BLOCK
    nb = block_exp.shape[0]
    assert nb >= GATHER_DEPTH
    tok3 = buf_tok.reshape(nb, 1, R)
    idx_spec = lambda a: pl.BlockSpec((1, 1, R), lambda i, e: (jnp.minimum(i + a, nb - 1), 0, 0),
                                      memory_space=pltpu.SMEM)
    return pl.pallas_call(
        _experts_kernel,
        grid_spec=pltpu.PrefetchScalarGridSpec(
            num_scalar_prefetch=1,
            grid=(nb,),
            in_specs=[idx_spec(a) for a in range(GATHER_DEPTH)] + [
                pl.BlockSpec(memory_space=pl.ANY),
                pl.BlockSpec((1, D, EXPERT_HIDDEN), lambda i, e: (e[i], 0, 0)),
                pl.BlockSpec((1, D, EXPERT_HIDDEN), lambda i, e: (e[i], 0, 0)),
                pl.BlockSpec((1, EXPERT_HIDDEN, D), lambda i, e: (e[i], 0, 0)),
            ],
            out_specs=pl.BlockSpec((R, D), lambda i, e: (i, 0)),
            scratch_shapes=[pltpu.VMEM((GATHER_DEPTH, R, D), F32),
                            pltpu.SemaphoreType.DMA((GATHER_DEPTH,))],
        ),
        out_shape=jax.ShapeDtypeStruct((nb * R, D), F32),
        compiler_params=_params(("arbitrary",)),
        name="experts",
    )(block_exp, *([tok3] * GATHER_DEPTH), h_flat, wg, wu, wd)


def _combine_ln2_kernel(pos_ref, pos_next_ref, h_ref, gate_ref, y_hbm, l2g_ref, l2b_ref, o_ref,
                        ybuf, sem):
    i = pl.program_id(0)
    n = pl.num_programs(0)
    T = OUT_TILE
    R = TOP_K * T
    slot = i % 2

    @pl.when(i == 0)
    def _():
        _start_row_gather(pos_ref, y_hbm, ybuf.at[0], sem.at[0], R)

    @pl.when(i + 1 < n)
    def _():
        _start_row_gather(pos_next_ref, y_hbm, ybuf.at[1 - slot], sem.at[1 - slot], R)

    _wait_row_gather(y_hbm, ybuf.at[slot], sem.at[slot], R)
    gates = gate_ref[...]
    moe = ybuf[slot, 0:T, :] * gates[:, 0:1] + ybuf[slot, T:R, :] * gates[:, 1:2]
    o_ref[...] = _layernorm(ALPHA * h_ref[...] + moe, l2g_ref[...], l2b_ref[...])


def _combine_ln2(h_flat, gates, pos, yb, l2g, l2b):
    n_tok, D = h_flat.shape
    T = OUT_TILE
    nt = n_tok // T
    idx_spec = lambda f: pl.BlockSpec((1, 1, TOP_K * T), f, memory_space=pltpu.SMEM)
    return pl.pallas_call(
        _combine_ln2_kernel,
        grid=(nt,),
        in_specs=[
            idx_spec(lambda i: (i, 0, 0)),
            idx_spec(lambda i: (jnp.minimum(i + 1, nt - 1), 0, 0)),
            pl.BlockSpec((T, D), lambda i: (i, 0)),
            pl.BlockSpec((T, ROUTE_LANES), lambda i: (i, 0)),
            pl.BlockSpec(memory_space=pl.ANY),
            pl.BlockSpec((1, D), lambda i: (0, 0)),
            pl.BlockSpec((1, D), lambda i: (0, 0)),
        ],
        out_specs=pl.BlockSpec((T, D), lambda i: (i, 0)),
        out_shape=jax.ShapeDtypeStruct((n_tok, D), F32),
        scratch_shapes=[pltpu.VMEM((2, TOP_K * T, D), F32), pltpu.SemaphoreType.DMA((2,))],
        compiler_params=_params(("arbitrary",)),
        name="combine_ln2",
    )(pos, pos, h_flat, gates, yb, l2g, l2b)


def _dispatch_plan(expert):
    n_tok = expert.shape[0]
    A = n_tok * TOP_K
    e_flat = expert.reshape(A)
    tok_flat = jnp.repeat(jnp.arange(n_tok, dtype=jnp.int32), TOP_K)
    order = jnp.argsort(e_flat)
    e_s, tok_s = e_flat[order], tok_flat[order]
    counts = jnp.zeros((N_EXPERTS,), jnp.int32).at[e_flat].add(1)
    padded = (counts + MOE_BLOCK - 1) // MOE_BLOCK * MOE_BLOCK
    starts = jnp.cumsum(counts) - counts
    pends = jnp.cumsum(padded)
    pstarts = pends - padded
    dest = pstarts[e_s] + jnp.arange(A, dtype=jnp.int32) - starts[e_s]
    nb = A // MOE_BLOCK + N_EXPERTS
    buf_tok = jnp.zeros((nb * MOE_BLOCK,), jnp.int32).at[dest].set(tok_s)
    block_start = jnp.arange(nb, dtype=jnp.int32) * MOE_BLOCK
    block_exp = jnp.minimum(jnp.sum(block_start[:, None] >= pends[None, :], axis=1),
                            N_EXPERTS - 1).astype(jnp.int32)
    pos = jnp.zeros((A,), jnp.int32).at[order].set(dest).reshape(n_tok, TOP_K)
    return buf_tok, block_exp, pos


def _trunk(x, w):
    B, S, D = x.shape
    qk, vt, rw = _in_proj(x, w["wqk"], w["wvt"], w["wrw"])
    att = _attention(qk, vt, w["att_lambda"], w["subln_g"])
    r, a, lw, kd, kka, rvt, g, bonus = _rwkv_pre(
        rw, w["mu"], w["w0cat"], w["wup_blk"], w["a0cat"], w["aup_blk"], w["gup"],
        w["k_k"], w["k_a"], w["r_k"], w["blk"])
    yt = _rwkv_scan(r, a, lw, kd, kka, rvt)
    h, expert, gates = _mix_ln1(x, att, yt, g, bonus, w["gn_g"], w["gn_b"], w["woa"], w["wob"],
                                w["l1g"], w["l1b"], w["wr_hi"], w["wr_lo"], w["br"])
    n_tok = B * S
    h_flat = h.reshape(n_tok, D)
    buf_tok, block_exp, pos = _dispatch_plan(expert.reshape(n_tok, ROUTE_LANES)[:, :TOP_K])
    yb = _experts(h_flat, buf_tok, block_exp, w["wg"], w["wu"], w["wd"])
    T = OUT_TILE
    pos_tiles = pos.reshape(n_tok // T, T, TOP_K).transpose(0, 2, 1).reshape(n_tok // T, 1, TOP_K * T)
    out = _combine_ln2(h_flat, gates.reshape(n_tok, ROUTE_LANES), pos_tiles, yb, w["l2g"], w["l2b"])
    return out.reshape(B, S, D)


def _block_diag2(m):
    z = jnp.zeros_like(m[0])
    return jnp.concatenate([jnp.concatenate([m[0], z], axis=1),
                            jnp.concatenate([z, m[1]], axis=1)], axis=0)


def kernel(x_prompt, x_sample, w_in, att_lambda, att_subln_g, rwkv_mu, rwkv_w0, rwkv_w_up, rwkv_a0, rwkv_a_up, rwkv_g_up, rwkv_k_k, rwkv_k_a, rwkv_r_k, rwkv_ln_g, rwkv_ln_b, w_out, ln1_g, ln1_b, router_g_w, router_g_b, router_e_w, router_e_b, exp_w_gate, exp_w_up, exp_w_down, ln2_g, ln2_b):
    assert w_in.shape[0] == DEPTH
    W = RWKV_WIDTH
    D = D_MODEL
    win = w_in[0]
    head = jnp.arange(W, dtype=jnp.int32) // RWKV_HEAD_DIM
    n_route = N_GROUPS + N_EXPERTS
    wr = jnp.concatenate([router_g_w[0],
                          jnp.transpose(router_e_w[0], (1, 0, 2)).reshape(D, N_EXPERTS)], axis=1)
    br = jnp.concatenate([router_g_b[0], router_e_b[0].reshape(N_EXPERTS)])
    wr_pad = jnp.pad(wr, ((0, 0), (0, ROUTE_LANES - n_route)))
    wr_hi = wr_pad.astype(BF16)
    w = {
        "wqk": win[:, :QK_WIDTH].astype(BF16),
        "wvt": win[:, QK_WIDTH:QK_WIDTH + ATT_WIDTH].T.astype(BF16),
        "wrw": win[:, QK_WIDTH + ATT_WIDTH:].astype(BF16),
        "att_lambda": att_lambda[0],
        "subln_g": att_subln_g[0],
        "mu": rwkv_mu[0],
        "w0cat": rwkv_w0[0].reshape(1, 2 * W),
        "wup_blk": _block_diag2(rwkv_w_up[0]).astype(BF16),
        "a0cat": rwkv_a0[0].reshape(1, 2 * W),
        "aup_blk": _block_diag2(rwkv_a_up[0]).astype(BF16),
        "gup": rwkv_g_up[0].astype(BF16),
        "k_k": rwkv_k_k[0].reshape(1, W),
        "k_a": rwkv_k_a[0].reshape(1, W),
        "r_k": rwkv_r_k[0].reshape(1, W),
        "blk": (head[:, None] == head[None, :]).astype(BF16),
        "gn_g": rwkv_ln_g[0].reshape(W, 1),
        "gn_b": rwkv_ln_b[0].reshape(W, 1),
        "woa": w_out[0][:ATT_WIDTH].astype(BF16),
        "wob": w_out[0][ATT_WIDTH:].astype(BF16),
        "l1g": ln1_g[0].reshape(1, D),
        "l1b": ln1_b[0].reshape(1, D),
        "wr_hi": wr_hi,
        "wr_lo": (wr_pad - wr_hi.astype(F32)).astype(BF16),
        "br": jnp.pad(br, (0, ROUTE_LANES - n_route)).reshape(1, ROUTE_LANES),
        "wg": exp_w_gate[0].astype(BF16),
        "wu": exp_w_up[0].astype(BF16),
        "wd": exp_w_down[0].astype(BF16),
        "l2g": ln2_g[0].reshape(1, D),
        "l2b": ln2_b[0].reshape(1, D),
    }
    return (_trunk(x_prompt, w), _trunk(x_sample, w))
```

```python
import functools
import math

import jax
import jax.numpy as jnp
from jax import lax
from jax.experimental import pallas as pl
from jax.experimental.pallas import tpu as pltpu

F32 = jnp.float32
BF16 = jnp.bfloat16

D_MODEL = 1024
ATT_HEADS = 4
ATT_QK_DIM = 64
ATT_V_DIM = 128
ATT_WIDTH = ATT_HEADS * ATT_V_DIM
QK_WIDTH = 2 * ATT_HEADS * 2 * ATT_QK_DIM
RWKV_HEADS = 8
RWKV_HEAD_DIM = 64
RWKV_WIDTH = RWKV_HEADS * RWKV_HEAD_DIM
LORA_DECAY = 64
LORA_AAA = 64
LORA_GATE = 128
RWKV_IN = 3 * RWKV_WIDTH + 2 * LORA_DECAY + 2 * LORA_AAA + LORA_GATE
N_GROUPS = 4
EXPERTS_PER_GROUP = 8
N_EXPERTS = N_GROUPS * EXPERTS_PER_GROUP
TOP_K = 2
EXPERT_HIDDEN = 512
DEPTH = 1
ALPHA = (2.0 * DEPTH) ** 0.25
LN_EPS = 1e-5
RMS_EPS = 1e-5
RWKV_GN_EPS = 64e-5
LAM_INIT = 0.8 - 0.6 * math.exp(-0.3 * 0)

LANES = 128
ROUTE_LANES = 128
VMEM_LIMIT = 56 * 1024 * 1024

PROJ_TILE = 512
ATT_QB = 128
ATT_KB = 512
PRE_TILE = 256
CHUNK = 128
MIX_TILE = 256
OUT_TILE = 256
EXPERT_ROWS = 512
NEG_BIG = -1e30
LOG2E = math.log2(math.e)
Q_SCALE = ATT_QK_DIM ** -0.5 * LOG2E


def _bdot(a, b):
    return jnp.dot(a.astype(BF16), b.astype(BF16), preferred_element_type=F32)


def _bdot_nt(a, b):
    return lax.dot_general(a.astype(BF16), b.astype(BF16), (((1,), (1,)), ((), ())),
                           preferred_element_type=F32)


def _split3(x):
    hi = x.astype(BF16)
    r1 = x - hi.astype(F32)
    mid = r1.astype(BF16)
    lo = (r1 - mid.astype(F32)).astype(BF16)
    return hi, mid, lo


def _params(sem):
    return pltpu.CompilerParams(dimension_semantics=sem, vmem_limit_bytes=VMEM_LIMIT)


def _in_proj_kernel(x_ref, wqk_ref, wvt_ref, wrw_ref, qk_ref, vt_ref, rw_ref):
    xb = x_ref[0].astype(BF16)
    qk = jnp.dot(xb, wqk_ref[...], preferred_element_type=F32)
    qk_ref[0, :, :QK_WIDTH // 2] = (qk[:, :QK_WIDTH // 2] * Q_SCALE).astype(BF16)
    qk_ref[0, :, QK_WIDTH // 2:] = qk[:, QK_WIDTH // 2:].astype(BF16)
    vt_ref[0] = lax.dot_general(wvt_ref[...], xb, (((1,), (1,)), ((), ())),
                                preferred_element_type=F32).astype(BF16)
    rw_ref[0] = jnp.dot(xb, wrw_ref[...], preferred_element_type=F32)


def _in_proj(x, wqk, wvt, wrw):
    B, S, D = x.shape
    T = PROJ_TILE
    return pl.pallas_call(
        _in_proj_kernel,
        grid=(B, S // T),
        in_specs=[
            pl.BlockSpec((1, T, D), lambda b, i: (b, i, 0)),
            pl.BlockSpec((D, QK_WIDTH), lambda b, i: (0, 0)),
            pl.BlockSpec((ATT_WIDTH, D), lambda b, i: (0, 0)),
            pl.BlockSpec((D, RWKV_IN), lambda b, i: (0, 0)),
        ],
        out_specs=[
            pl.BlockSpec((1, T, QK_WIDTH), lambda b, i: (b, i, 0)),
            pl.BlockSpec((1, ATT_WIDTH, T), lambda b, i: (b, 0, i)),
            pl.BlockSpec((1, T, RWKV_IN), lambda b, i: (b, i, 0)),
        ],
        out_shape=[
            jax.ShapeDtypeStruct((B, S, QK_WIDTH), BF16),
            jax.ShapeDtypeStruct((B, ATT_WIDTH, S), BF16),
            jax.ShapeDtypeStruct((B, S, RWKV_IN), F32),
        ],
        compiler_params=_params(("parallel", "parallel")),
        name="in_proj",
    )(x, wqk, wvt, wrw)


def _stack_components(q):
    lane = lax.broadcasted_iota(jnp.int32, q.shape, 1)
    zero = jnp.zeros_like(q)
    return jnp.concatenate([jnp.where(lane < ATT_QK_DIM, q, zero),
                            jnp.where(lane >= ATT_QK_DIM, q, zero)], axis=0)


def _score_tile(k_ref, bias_ref, qq, win, j):
    KB = ATT_KB
    kt = k_ref[0, j * KB:(j + 1) * KB, :]
    s = lax.dot_general(kt, qq, (((1,), (1,)), ((), ())), preferred_element_type=F32)
    bias = bias_ref[pl.ds(win + j * KB, KB), :]
    return s - jnp.concatenate([bias, bias], axis=1)


def _scores_and_values(k_ref, vt_ref, bias_ref, qq_new, win_new, s_new, s_old, m_old, seq):
    QB, KB = ATT_QB, ATT_KB
    m_new = jnp.full((1, 2 * QB), NEG_BIG, F32)
    l = jnp.zeros((1, 2 * QB), F32)
    acc = jnp.zeros((ATT_V_DIM, 2 * QB), F32)
    for j in range(seq // KB):
        p = jnp.exp2(s_old[j * KB:(j + 1) * KB, :] - m_old)
        s = _score_tile(k_ref, bias_ref, qq_new, win_new, j)
        s_new[j * KB:(j + 1) * KB, :] = s
        m_new = jnp.maximum(m_new, jnp.max(s, axis=0, keepdims=True))
        l = l + jnp.sum(p, axis=0, keepdims=True)
        vt = vt_ref[0, :, j * KB:(j + 1) * KB]
        acc = acc + jnp.dot(vt, p.astype(BF16), preferred_element_type=F32)
    return m_new, l, acc


def _attention_kernel(slopes_ref, q_ref, qn_ref, k_ref, vt_ref, lam_ref, g_ref, o_ref,
                      bias_ref, sa_ref, sb_ref, ma_ref, *, seq):
    h = pl.program_id(1)
    i = pl.program_id(2)
    QB, KB = ATT_QB, ATT_KB
    nq = seq // QB
    slope = slopes_ref[h]

    @pl.when(i == 0)
    def _():
        for c in range(2 * seq // KB):
            r = lax.broadcasted_iota(jnp.int32, (KB, QB), 0) + (c * KB - seq)
            qq_ = lax.broadcasted_iota(jnp.int32, (KB, QB), 1)
            bias_ref[c * KB:(c + 1) * KB, :] = slope * jnp.abs(r - qq_).astype(F32)
        qq0 = _stack_components(q_ref[0, :QB, :])
        m0 = jnp.full((1, 2 * QB), NEG_BIG, F32)
        for j in range(seq // KB):
            s = _score_tile(k_ref, bias_ref, qq0, seq, j)
            sa_ref[j * KB:(j + 1) * KB, :] = s
            m0 = jnp.maximum(m0, jnp.max(s, axis=0, keepdims=True))
        ma_ref[...] = m0

    lmb = lam_ref[...]
    lam = (jnp.exp(jnp.sum(lmb[0:1] * lmb[1:2], axis=1, keepdims=True))
           - jnp.exp(jnp.sum(lmb[2:3] * lmb[3:4], axis=1, keepdims=True)) + LAM_INIT)

    def finish(l, acc):
        o = acc[:, :QB] / l[:, :QB] - lam * (acc[:, QB:] / l[:, QB:])
        ms = jnp.mean(o * o, axis=0, keepdims=True)
        o = o * lax.rsqrt(ms + RMS_EPS) * g_ref[...] * (1.0 - LAM_INIT)
        return o.T.astype(BF16)

    win1 = pl.multiple_of(seq - (2 * i + 1) * QB, QB)
    m_b, l, acc = _scores_and_values(k_ref, vt_ref, bias_ref, _stack_components(q_ref[0, QB:, :]), win1,
                                     sb_ref, sa_ref, ma_ref[...], seq)
    o_ref[0, :QB, :] = finish(l, acc)
    win2 = pl.multiple_of(seq - jnp.minimum(2 * i + 2, nq - 1) * QB, QB)
    m_a, l, acc = _scores_and_values(k_ref, vt_ref, bias_ref, _stack_components(qn_ref[0]), win2,
                                     sa_ref, sb_ref, m_b, seq)
    ma_ref[...] = m_a
    o_ref[0, QB:, :] = finish(l, acc)


def _attention(qk, vt, att_lambda, subln_g):
    B, S, _ = qk.shape
    H = ATT_HEADS
    nq = S // ATT_QB
    slopes = jnp.asarray([LOG2E * 2.0 ** (-8.0 * (i + 1) / H) for i in range(H)], F32)
    return pl.pallas_call(
        functools.partial(_attention_kernel, seq=S),
        grid_spec=pltpu.PrefetchScalarGridSpec(
            num_scalar_prefetch=1,
            grid=(B, H, nq // 2),
            in_specs=[
                pl.BlockSpec((1, 2 * ATT_QB, LANES), lambda b, h, i, s: (b, i, h)),
                pl.BlockSpec((1, ATT_QB, LANES),
                             lambda b, h, i, s: (b, jnp.minimum(2 * i + 2, nq - 1), h)),
                pl.BlockSpec((1, S, LANES), lambda b, h, i, s: (b, 0, H + h)),
                pl.BlockSpec((1, ATT_V_DIM, S), lambda b, h, i, s: (b, h, 0)),
                pl.BlockSpec((4, ATT_QK_DIM), lambda b, h, i, s: (0, 0)),
                pl.BlockSpec((ATT_V_DIM, 1), lambda b, h, i, s: (0, 0)),
            ],
            out_specs=pl.BlockSpec((1, 2 * ATT_QB, LANES), lambda b, h, i, s: (b, i, h)),
            scratch_shapes=[pltpu.VMEM((2 * S, ATT_QB), F32),
                            pltpu.VMEM((S, 2 * ATT_QB), F32),
                            pltpu.VMEM((S, 2 * ATT_QB), F32),
                            pltpu.VMEM((1, 2 * ATT_QB), F32)],
        ),
        out_shape=jax.ShapeDtypeStruct((B, S, ATT_WIDTH), BF16),
        compiler_params=_params(("parallel", "parallel", "arbitrary")),
        name="attention",
    )(slopes, qk, qk, qk, vt, att_lambda, subln_g.reshape(ATT_V_DIM, 1))


def _head_sum(x, blk):
    hi, mid, lo = _split3(x)
    return (jnp.dot(hi, blk, preferred_element_type=F32) + jnp.dot(mid, blk, preferred_element_type=F32)
            + jnp.dot(lo, blk, preferred_element_type=F32))


def _rwkv_pre_kernel(p_ref, prev_ref, next_ref, mu_ref, w0_ref, wup_ref, a0_ref, aup_ref, gup_ref,
                     kk_ref, ka_ref, rk_ref, blk_ref,
                     r_ref, a_ref, lw_ref, kd_ref, kka_ref, vt_ref, g_ref, bonus_ref):
    i = pl.program_id(1)
    n = pl.num_programs(1)
    T = PRE_TILE
    W = RWKV_WIDTH
    p = p_ref[0]
    row = lax.broadcasted_iota(jnp.int32, p.shape, 0)
    prev_row = jnp.where(i > 0, prev_ref[0, 7:8, :], 0.0)
    next_row = jnp.where(i < n - 1, next_ref[0, 0:1, :], 0.0)
    prev = jnp.where(row == 0, prev_row, pltpu.roll(p, 1, axis=0))
    nxt = jnp.where(row == T - 1, next_row, pltpu.roll(p, T - 1, axis=0))
    p = p + mu_ref[0:1, :] * (prev - p) + mu_ref[1:2, :] * (nxt - p)

    r = p[:, 0:W]
    k = p[:, W:2 * W]
    v = p[:, 2 * W:3 * W]
    c = 3 * W
    wd = p[:, c:c + 2 * LORA_DECAY]
    ad = p[:, c + 2 * LORA_DECAY:c + 2 * LORA_DECAY + 2 * LORA_AAA]
    gd = p[:, c + 2 * LORA_DECAY + 2 * LORA_AAA:]

    wl = w0_ref[...] + _bdot(jnp.tanh(wd), wup_ref[...])
    lw = -math.exp(-0.5) * jax.nn.sigmoid(wl)
    av = jax.nn.sigmoid(a0_ref[...] + _bdot(ad, aup_ref[...]))
    g_ref[0] = _bdot(jax.nn.sigmoid(gd), gup_ref[...])

    blk = blk_ref[...]
    kkr = k * kk_ref[...]
    kk = kkr / jnp.maximum(jnp.sqrt(_head_sum(kkr * kkr, blk)), 1e-12)
    ksum = jnp.zeros_like(k)
    for d in range(2):
        a_d = av[:, d * W:(d + 1) * W]
        k_d = k * (1.0 + (a_d - 1.0) * ka_ref[...])
        ksum = ksum + k_d
        lw_ref[d, 0] = lw[:, d * W:(d + 1) * W]
        kd_ref[d, 0] = k_d
        kka_ref[d, 0] = kk * a_d
    r_ref[0] = r
    a_ref[0] = -kk
    vt_ref[0] = v.T
    bonus_ref[0] = _head_sum(r * ksum * rk_ref[...], blk) * v


def _rwkv_pre(rw, mu, w0cat, wup_blk, a0cat, aup_blk, gup, k_k, k_a, r_k, blk):
    B, S, _ = rw.shape
    T = PRE_TILE
    W = RWKV_WIDTH
    nb8 = S // 8
    full = lambda shape: pl.BlockSpec(shape, lambda b, i: (0,) * len(shape))
    row_spec = pl.BlockSpec((1, T, W), lambda b, i: (b, i, 0))
    dir_spec = pl.BlockSpec((2, 1, T, W), lambda b, i: (0, b, i, 0))
    row_shape = jax.ShapeDtypeStruct((B, S, W), F32)
    dir_shape = jax.ShapeDtypeStruct((2, B, S, W), F32)
    return pl.pallas_call(
        _rwkv_pre_kernel,
        grid=(B, S // T),
        in_specs=[
            pl.BlockSpec((1, T, RWKV_IN), lambda b, i: (b, i, 0)),
            pl.BlockSpec((1, 8, RWKV_IN), lambda b, i: (b, jnp.maximum(i * (T // 8) - 1, 0), 0)),
            pl.BlockSpec((1, 8, RWKV_IN), lambda b, i: (b, jnp.minimum((i + 1) * (T // 8), nb8 - 1), 0)),
            full((2, RWKV_IN)), full((1, 2 * W)), full((2 * LORA_DECAY, 2 * W)),
            full((1, 2 * W)), full((2 * LORA_AAA, 2 * W)), full((LORA_GATE, W)),
            full((1, W)), full((1, W)), full((1, W)), full((W, W)),
        ],
        out_specs=[row_spec, row_spec, dir_spec, dir_spec, dir_spec,
                   pl.BlockSpec((1, W, T), lambda b, i: (b, 0, i)), row_spec, row_spec],
        out_shape=[row_shape, row_shape, dir_shape, dir_shape, dir_shape,
                   jax.ShapeDtypeStruct((B, W, S), F32), row_shape, row_shape],
        compiler_params=_params(("parallel", "parallel")),
        name="rwkv_pre",
    )(rw, rw, rw, mu, w0cat, wup_blk, a0cat, aup_blk, gup, k_k, k_a, r_k, blk)


def _rwkv_scan_kernel(r_ref, a_ref, lw_ref, kd_ref, kka_ref, vt_ref, yt_ref, state_ref):
    d = pl.program_id(1)
    c = pl.program_id(2)
    C = CHUNK
    N = RWKV_HEAD_DIM

    @pl.when(c == 0)
    def _():
        state_ref[...] = jnp.zeros_like(state_ref)

    row = lax.broadcasted_iota(jnp.int32, (C, C), 0)
    col = lax.broadcasted_iota(jnp.int32, (C, C), 1)
    order = (row - col) * (1 - 2 * d)
    strict = order > 0
    incl = order >= 0
    tri = jnp.where(incl, 1.0, 0.0).astype(BF16)
    eye = jnp.where(row == col, 1.0, 0.0)

    lw = lw_ref[0, 0]
    k = kd_ref[0, 0]
    b = kka_ref[0, 0]
    lw_hi, lw_mid, lw_lo = _split3(lw)
    cs = (jnp.dot(tri, lw_hi, preferred_element_type=F32) + jnp.dot(tri, lw_mid, preferred_element_type=F32)
          + jnp.dot(tri, lw_lo, preferred_element_type=F32))
    tot = jnp.sum(lw, axis=0, keepdims=True)
    g_inv = jnp.exp(-cs)
    g_last = jnp.exp(tot - cs)
    g_tot = jnp.exp(tot)
    a_t = (a_ref[0] * jnp.exp(cs - lw)).astype(BF16)
    r_t = (r_ref[0] * jnp.exp(cs)).astype(BF16)
    lhs = jnp.concatenate([a_t, r_t], axis=0)
    rhs = jnp.concatenate([(b * g_inv).astype(BF16), (k * g_inv).astype(BF16)], axis=0)
    b_end = (b * g_last).astype(BF16)
    k_end = (k * g_last).astype(BF16)

    heads = range(RWKV_HEADS)
    sl = [slice(h * N, (h + 1) * N) for h in heads]
    vt = [vt_ref[0, sl[h], :].astype(BF16) for h in heads]
    s0 = [state_ref[h] for h in heads]
    gram = [_bdot_nt(lhs[:, sl[h]], rhs[:, sl[h]]) for h in heads]
    l_ab = [jnp.where(strict, gram[h][:C, :C], 0.0) for h in heads]
    l_ak = [jnp.where(strict, gram[h][:C, C:], 0.0).astype(BF16) for h in heads]
    m_rb = [jnp.where(incl, gram[h][C:, :C], 0.0).astype(BF16) for h in heads]
    m_rk = [jnp.where(incl, gram[h][C:, C:], 0.0).astype(BF16) for h in heads]

    levels = int(math.log2(C))
    t = [eye + l_ab[h] for h in heads]
    pw = [_bdot(l_ab[h], l_ab[h]) for h in heads]
    for j in range(1, levels):
        pb = [pw[h].astype(BF16) for h in heads]
        if j < levels - 1:
            both = [_bdot(pb[h], jnp.concatenate([pb[h], t[h].astype(BF16)], axis=1)) for h in heads]
            pw = [both[h][:, :C] for h in heads]
            t = [t[h] + both[h][:, C:] for h in heads]
        else:
            t = [t[h] + _bdot(pb[h], t[h]) for h in heads]

    tb = [t[h].astype(BF16) for h in heads]
    xw = [_bdot(tb[h], jnp.concatenate([l_ak[h], a_t[:, sl[h]]], axis=1)) for h in heads]
    s0b = [s0[h].astype(BF16) for h in heads]
    ut = [_bdot_nt(jnp.concatenate([vt[h], s0b[h]], axis=1), xw[h]).astype(BF16)
          for h in heads]
    for h in heads:
        yt_ref[0, 0, sl[h], :] = _bdot_nt(
            jnp.concatenate([ut[h], vt[h], s0b[h]], axis=1),
            jnp.concatenate([m_rb[h], m_rk[h], r_t[:, sl[h]]], axis=1))
    for h in heads:
        state_ref[h] = s0[h] * g_tot[:, sl[h]] + _bdot(
            jnp.concatenate([ut[h], vt[h]], axis=1),
            jnp.concatenate([b_end[:, sl[h]], k_end[:, sl[h]]], axis=0))


def _rwkv_scan(r, a, lw, kd, kka, vt):
    B, S, W = r.shape
    C = CHUNK
    nc = S // C
    cidx = lambda d, c: c + d * (nc - 1 - 2 * c)
    row_spec = pl.BlockSpec((1, C, W), lambda b, d, c: (b, cidx(d, c), 0))
    dir_spec = pl.BlockSpec((1, 1, C, W), lambda b, d, c: (d, b, cidx(d, c), 0))
    return pl.pallas_call(
        _rwkv_scan_kernel,
        grid=(B, 2, nc),
        in_specs=[row_spec, row_spec, dir_spec, dir_spec, dir_spec,
                  pl.BlockSpec((1, W, C), lambda b, d, c: (b, 0, cidx(d, c)))],
        out_specs=pl.BlockSpec((1, 1, W, C), lambda b, d, c: (d, b, 0, cidx(d, c))),
        out_shape=jax.ShapeDtypeStruct((2, B, W, S), F32),
        scratch_shapes=[pltpu.VMEM((RWKV_HEADS, RWKV_HEAD_DIM, RWKV_HEAD_DIM), F32)],
        compiler_params=_params(("parallel", "parallel", "arbitrary")),
        name="rwkv_scan",
    )(r, a, lw, kd, kka, vt)


def _layernorm(z, g, b):
    mu = jnp.mean(z, axis=-1, keepdims=True)
    zc = z - mu
    var = jnp.mean(zc * zc, axis=-1, keepdims=True)
    return zc * lax.rsqrt(var + LN_EPS) * g + b


def _first_lane_where(mask, lane):
    return jnp.min(jnp.where(mask, lane, ROUTE_LANES), axis=1, keepdims=True)


def _mix_ln1_kernel(x_ref, att_ref, ytf_ref, ytb_ref, g_ref, bonus_ref, gng_ref, gnb_ref,
                    woa_ref, wob_ref, l1g_ref, l1b_ref, wrh_ref, wrl_ref, br_ref,
                    h_ref, exp_ref, gate_ref, cnt_ref, base_ref):
    T = MIX_TILE
    N = RWKV_HEAD_DIM
    yt = ytf_ref[0, 0] + ytb_ref[0, 0]
    y3 = yt.reshape(RWKV_HEADS, N, T)
    mu = jnp.mean(y3, axis=1, keepdims=True)
    yc = y3 - mu
    var = jnp.mean(yc * yc, axis=1, keepdims=True)
    yn = (yc * lax.rsqrt(var + RWKV_GN_EPS)).reshape(RWKV_WIDTH, T)
    yn = yn * gng_ref[...] + gnb_ref[...]
    tm = (yn.T + bonus_ref[0]) * g_ref[0]
    mix = (jnp.dot(att_ref[0], woa_ref[...], preferred_element_type=F32)
           + _bdot(tm, wob_ref[...]))
    h = _layernorm(ALPHA * x_ref[0] + mix, l1g_ref[...], l1b_ref[...])
    h_ref[0] = h

    h_hi = h.astype(BF16)
    h_lo = (h - h_hi.astype(F32)).astype(BF16)
    logits = (jnp.dot(h_hi, wrh_ref[...], preferred_element_type=F32)
              + jnp.dot(h_lo, wrh_ref[...], preferred_element_type=F32)
              + jnp.dot(h_hi, wrl_ref[...], preferred_element_type=F32)) + br_ref[...]
    lane = lax.broadcasted_iota(jnp.int32, (T, ROUTE_LANES), 1)
    neg = jnp.asarray(-jnp.inf, F32)
    gl = jnp.where(lane < N_GROUPS, logits, neg)
    gmax = jnp.max(gl, axis=1, keepdims=True)
    grp = _first_lane_where(gl == gmax, lane)
    g1 = 1.0 / jnp.sum(jnp.exp(gl - gmax), axis=1, keepdims=True)
    lo = N_GROUPS + EXPERTS_PER_GROUP * grp
    sel = jnp.where((lane >= lo) & (lane < lo + EXPERTS_PER_GROUP), logits, neg)
    v1 = jnp.max(sel, axis=1, keepdims=True)
    i1 = _first_lane_where(sel == v1, lane)
    sel2 = jnp.where(lane == i1, neg, sel)
    v2 = jnp.max(sel2, axis=1, keepdims=True)
    i2 = _first_lane_where(sel2 == v2, lane)
    e2 = jnp.exp(v2 - v1)
    den = 1.0 + e2
    gate1 = g1 * (1.0 / den)
    gate2 = g1 * (e2 / den)
    gate_ref[0] = jnp.where(lane == 0, gate1, jnp.where(lane == 1, gate2, 0.0))

    @pl.when((pl.program_id(0) == 0) & (pl.program_id(1) == 0))
    def _():
        base_ref[...] = jnp.zeros_like(base_ref)

    e1 = i1 - N_GROUPS
    e2 = i2 - N_GROUPS
    oh1 = jnp.where(lane == e1, 1.0, 0.0)
    oh2 = jnp.where(lane == e2, 1.0, 0.0)
    cnt = oh1 + oh2
    ti = lax.broadcasted_iota(jnp.int32, (T, T), 0)
    tj = lax.broadcasted_iota(jnp.int32, (T, T), 1)
    earlier = jnp.where(tj < ti, 1.0, 0.0).astype(BF16)
    before = jnp.dot(earlier, cnt.astype(BF16), preferred_element_type=F32) + base_ref[...]
    rank1 = jnp.sum(oh1 * before, axis=1, keepdims=True).astype(jnp.int32)
    rank2 = jnp.sum(oh2 * before, axis=1, keepdims=True).astype(jnp.int32)
    total = base_ref[...] + jnp.sum(cnt, axis=0, keepdims=True)
    base_ref[...] = total
    cnt_ref[...] = total
    exp_ref[0] = jnp.where(lane == 0, e1, jnp.where(lane == 1, e2, jnp.where(
        lane == 2, rank1, jnp.where(lane == 3, rank2, 0))))


def _mix_ln1(x, att, yt, g, bonus, gn_g, gn_b, woa, wob, l1g, l1b, wr_hi, wr_lo, br):
    B, S, D = x.shape
    T = MIX_TILE
    W = RWKV_WIDTH
    full = lambda shape: pl.BlockSpec(shape, lambda b, i: (0,) * len(shape))
    row = lambda width: pl.BlockSpec((1, T, width), lambda b, i: (b, i, 0))
    return pl.pallas_call(
        _mix_ln1_kernel,
        grid=(B, S // T),
        in_specs=[
            row(D), row(ATT_WIDTH),
            pl.BlockSpec((1, 1, W, T), lambda b, i: (0, b, 0, i)),
            pl.BlockSpec((1, 1, W, T), lambda b, i: (1, b, 0, i)),
            row(W), row(W), full((W, 1)), full((W, 1)),
            full((ATT_WIDTH, D)), full((W, D)), full((1, D)), full((1, D)),
            full((D, ROUTE_LANES)), full((D, ROUTE_LANES)), full((1, ROUTE_LANES)),
        ],
        out_specs=[row(D), row(ROUTE_LANES), row(ROUTE_LANES), full((1, ROUTE_LANES))],
        out_shape=[jax.ShapeDtypeStruct((B, S, D), F32),
                   jax.ShapeDtypeStruct((B, S, ROUTE_LANES), jnp.int32),
                   jax.ShapeDtypeStruct((B, S, ROUTE_LANES), F32),
                   jax.ShapeDtypeStruct((1, ROUTE_LANES), F32)],
        scratch_shapes=[pltpu.VMEM((1, ROUTE_LANES), F32)],
        compiler_params=_params(("arbitrary", "arbitrary")),
        name="mix_ln1",
    )(x, att, yt, yt, g, bonus, gn_g, gn_b, woa, wob, l1g, l1b, wr_hi, wr_lo, br)


def _start_row_gather(idx_ref, src_hbm, dst, sem, rows):
    for r in range(rows):
        pltpu.make_async_copy(src_hbm.at[pl.ds(idx_ref[0, 0, r], 1), :],
                              dst.at[pl.ds(r, 1), :], sem).start()


def _wait_row_gather(src_hbm, dst, sem, rows):
    pltpu.make_async_copy(src_hbm.at[pl.ds(0, rows), :], dst, sem).wait()


def _dispatch_kernel(pos_ref, h_ref, xs_in_hbm, xs_hbm, sem):
    del xs_in_hbm
    T = OUT_TILE
    for k in range(TOP_K):
        for r in range(T):
            pltpu.make_async_copy(h_ref.at[pl.ds(r, 1), :],
                                  xs_hbm.at[pl.ds(pos_ref[0, 0, k * T + r], 1), :], sem).start()
    for k in range(TOP_K):
        pltpu.make_async_copy(h_ref, xs_hbm.at[pl.ds(0, T), :], sem).wait()


def _dispatch(h_flat, pos_tiles, n_rows):
    n_tok, D = h_flat.shape
    T = OUT_TILE
    nt = n_tok // T
    return pl.pallas_call(
        _dispatch_kernel,
        grid=(nt,),
        in_specs=[
            pl.BlockSpec((1, 1, TOP_K * T), lambda i: (i, 0, 0), memory_space=pltpu.SMEM),
            pl.BlockSpec((T, D), lambda i: (i, 0)),
            pl.BlockSpec(memory_space=pl.ANY),
        ],
        out_specs=pl.BlockSpec(memory_space=pl.ANY),
        out_shape=jax.ShapeDtypeStruct((n_rows, D), F32),
        scratch_shapes=[pltpu.SemaphoreType.DMA(())],
        input_output_aliases={2: 0},
        compiler_params=_params(("arbitrary",)),
        name="dispatch",
    )(pos_tiles, h_flat, jnp.zeros((n_rows, D), F32))


def _experts_kernel(bexp_ref, x_ref, wg_ref, wu_ref, wd_ref, y_ref):
    del bexp_ref
    xb = x_ref[...].astype(BF16)
    hid = (jax.nn.silu(jnp.dot(xb, wg_ref[0], preferred_element_type=F32))
           * jnp.dot(xb, wu_ref[0], preferred_element_type=F32))
    y_ref[...] = _bdot(hid, wd_ref[0])


def _experts(xs, block_exp, wg, wu, wd):
    D = D_MODEL
    R = EXPERT_ROWS
    nb = block_exp.shape[0]
    return pl.pallas_call(
        _experts_kernel,
        grid_spec=pltpu.PrefetchScalarGridSpec(
            num_scalar_prefetch=1,
            grid=(nb,),
            in_specs=[
                pl.BlockSpec((R, D), lambda i, e: (i, 0)),
                pl.BlockSpec((1, D, EXPERT_HIDDEN), lambda i, e: (e[i], 0, 0)),
                pl.BlockSpec((1, D, EXPERT_HIDDEN), lambda i, e: (e[i], 0, 0)),
                pl.BlockSpec((1, EXPERT_HIDDEN, D), lambda i, e: (e[i], 0, 0)),
            ],
            out_specs=pl.BlockSpec((R, D), lambda i, e: (i, 0)),
        ),
        out_shape=jax.ShapeDtypeStruct((nb * R, D), F32),
        compiler_params=_params(("parallel",)),
        name="experts",
    )(block_exp, xs, wg, wu, wd)


def _combine_ln2_kernel(pos_ref, pos_next_ref, h_ref, gate_ref, y_hbm, l2g_ref, l2b_ref, o_ref,
                        ybuf, sem):
    i = pl.program_id(0)
    n = pl.num_programs(0)
    T = OUT_TILE
    R = TOP_K * T
    slot = i % 2

    @pl.when(i == 0)
    def _():
        _start_row_gather(pos_ref, y_hbm, ybuf.at[0], sem.at[0], R)

    @pl.when(i + 1 < n)
    def _():
        _start_row_gather(pos_next_ref, y_hbm, ybuf.at[1 - slot], sem.at[1 - slot], R)

    _wait_row_gather(y_hbm, ybuf.at[slot], sem.at[slot], R)
    gates = gate_ref[...]
    moe = ybuf[slot, 0:T, :] * gates[:, 0:1] + ybuf[slot, T:R, :] * gates[:, 1:2]
    o_ref[...] = _layernorm(ALPHA * h_ref[...] + moe, l2g_ref[...], l2b_ref[...])


def _combine_ln2(h_flat, gates, pos, yb, l2g, l2b):
    n_tok, D = h_flat.shape
    T = OUT_TILE
    nt = n_tok // T
    idx_spec = lambda f: pl.BlockSpec((1, 1, TOP_K * T), f, memory_space=pltpu.SMEM)
    return pl.pallas_call(
        _combine_ln2_kernel,
        grid=(nt,),
        in_specs=[
            idx_spec(lambda i: (i, 0, 0)),
            idx_spec(lambda i: (jnp.minimum(i + 1, nt - 1), 0, 0)),
            pl.BlockSpec((T, D), lambda i: (i, 0)),
            pl.BlockSpec((T, ROUTE_LANES), lambda i: (i, 0)),
            pl.BlockSpec(memory_space=pl.ANY),
            pl.BlockSpec((1, D), lambda i: (0, 0)),
            pl.BlockSpec((1, D), lambda i: (0, 0)),
        ],
        out_specs=pl.BlockSpec((T, D), lambda i: (i, 0)),
        out_shape=jax.ShapeDtypeStruct((n_tok, D), F32),
        scratch_shapes=[pltpu.VMEM((2, TOP_K * T, D), F32), pltpu.SemaphoreType.DMA((2,))],
        compiler_params=_params(("arbitrary",)),
        name="combine_ln2",
    )(pos, pos, h_flat, gates, yb, l2g, l2b)


def _dispatch_plan(expert, rank, counts):
    n_tok = expert.shape[0]
    R = EXPERT_ROWS
    padded = (counts + R - 1) // R * R
    pends = jnp.cumsum(padded)
    pstarts = pends - padded
    ids = jnp.arange(N_EXPERTS, dtype=jnp.int32)
    pos = rank + jnp.sum(jnp.where(expert[..., None] == ids, pstarts, 0), axis=-1)
    nb = n_tok * TOP_K // R + N_EXPERTS
    block_start = jnp.arange(nb, dtype=jnp.int32) * R
    block_exp = jnp.minimum(jnp.sum(block_start[:, None] >= pends[None, :], axis=1),
                            N_EXPERTS - 1).astype(jnp.int32)
    return pos.astype(jnp.int32), block_exp


def _trunk(x, w):
    B, S, D = x.shape
    qk, vt, rw = _in_proj(x, w["wqk"], w["wvt"], w["wrw"])
    att = _attention(qk, vt, w["att_lambda"], w["subln_g"])
    r, a, lw, kd, kka, rvt, g, bonus = _rwkv_pre(
        rw, w["mu"], w["w0cat"], w["wup_blk"], w["a0cat"], w["aup_blk"], w["gup"],
        w["k_k"], w["k_a"], w["r_k"], w["blk"])
    yt = _rwkv_scan(r, a, lw, kd, kka, rvt)
    h, route, gates, counts = _mix_ln1(x, att, yt, g, bonus, w["gn_g"], w["gn_b"], w["woa"], w["wob"],
                                       w["l1g"], w["l1b"], w["wr_hi"], w["wr_lo"], w["br"])
    n_tok = B * S
    h_flat = h.reshape(n_tok, D)
    route = route.reshape(n_tok, ROUTE_LANES)
    pos, block_exp = _dispatch_plan(route[:, :TOP_K], route[:, TOP_K:2 * TOP_K],
                                    counts[0, :N_EXPERTS].astype(jnp.int32))
    T = OUT_TILE
    pos_tiles = pos.reshape(n_tok // T, T, TOP_K).transpose(0, 2, 1).reshape(n_tok // T, 1, TOP_K * T)
    xs = _dispatch(h_flat, pos_tiles, block_exp.shape[0] * EXPERT_ROWS)
    yb = _experts(xs, block_exp, w["wg"], w["wu"], w["wd"])
    out = _combine_ln2(h_flat, gates.reshape(n_tok, ROUTE_LANES), pos_tiles, yb, w["l2g"], w["l2b"])
    return out.reshape(B, S, D)


def _block_diag2(m):
    z = jnp.zeros_like(m[0])
    return jnp.concatenate([jnp.concatenate([m[0], z], axis=1),
                            jnp.concatenate([z, m[1]], axis=1)], axis=0)


def kernel(x_prompt, x_sample, w_in, att_lambda, att_subln_g, rwkv_mu, rwkv_w0, rwkv_w_up, rwkv_a0, rwkv_a_up, rwkv_g_up, rwkv_k_k, rwkv_k_a, rwkv_r_k, rwkv_ln_g, rwkv_ln_b, w_out, ln1_g, ln1_b, router_g_w, router_g_b, router_e_w, router_e_b, exp_w_gate, exp_w_up, exp_w_down, ln2_g, ln2_b):
    assert w_in.shape[0] == DEPTH
    W = RWKV_WIDTH
    D = D_MODEL
    win = w_in[0]
    head = jnp.arange(W, dtype=jnp.int32) // RWKV_HEAD_DIM
    n_route = N_GROUPS + N_EXPERTS
    wr = jnp.concatenate([router_g_w[0],
                          jnp.transpose(router_e_w[0], (1, 0, 2)).reshape(D, N_EXPERTS)], axis=1)
    br = jnp.concatenate([router_g_b[0], router_e_b[0].reshape(N_EXPERTS)])
    wr_pad = jnp.pad(wr, ((0, 0), (0, ROUTE_LANES - n_route)))
    wr_hi = wr_pad.astype(BF16)
    w = {
        "wqk": win[:, :QK_WIDTH].astype(BF16),
        "wvt": win[:, QK_WIDTH:QK_WIDTH + ATT_WIDTH].T.astype(BF16),
        "wrw": win[:, QK_WIDTH + ATT_WIDTH:].astype(BF16),
        "att_lambda": att_lambda[0],
        "subln_g": att_subln_g[0],
        "mu": rwkv_mu[0],
        "w0cat": rwkv_w0[0].reshape(1, 2 * W),
        "wup_blk": _block_diag2(rwkv_w_up[0]).astype(BF16),
        "a0cat": rwkv_a0[0].reshape(1, 2 * W),
        "aup_blk": _block_diag2(rwkv_a_up[0]).astype(BF16),
        "gup": rwkv_g_up[0].astype(BF16),
        "k_k": rwkv_k_k[0].reshape(1, W),
        "k_a": rwkv_k_a[0].reshape(1, W),
        "r_k": rwkv_r_k[0].reshape(1, W),
        "blk": (head[:, None] == head[None, :]).astype(BF16),
        "gn_g": rwkv_ln_g[0].reshape(W, 1),
        "gn_b": rwkv_ln_b[0].reshape(W, 1),
        "woa": w_out[0][:ATT_WIDTH].astype(BF16),
        "wob": w_out[0][ATT_WIDTH:].astype(BF16),
        "l1g": ln1_g[0].reshape(1, D),
        "l1b": ln1_b[0].reshape(1, D),
        "wr_hi": wr_hi,
        "wr_lo": (wr_pad - wr_hi.astype(F32)).astype(BF16),
        "br": jnp.pad(br, (0, ROUTE_LANES - n_route)).reshape(1, ROUTE_LANES),
        "wg": exp_w_gate[0].astype(BF16),
        "wu": exp_w_up[0].astype(BF16),
        "wd": exp_w_down[0].astype(BF16),
        "l2g": ln2_g[0].reshape(1, D),
        "l2b": ln2_b[0].reshape(1, D),
    }
    return (_trunk(x_prompt, w), _trunk(x_sample, w))
```

```python
import functools
import math

import jax
import jax.numpy as jnp
from jax import lax
from jax.experimental import pallas as pl
from jax.experimental.pallas import tpu as pltpu

F32 = jnp.float32
BF16 = jnp.bfloat16

D_MODEL = 1024
ATT_HEADS = 4
ATT_QK_DIM = 64
ATT_V_DIM = 128
ATT_WIDTH = ATT_HEADS * ATT_V_DIM
QK_WIDTH = 2 * ATT_HEADS * 2 * ATT_QK_DIM
RWKV_HEADS = 8
RWKV_HEAD_DIM = 64
RWKV_WIDTH = RWKV_HEADS * RWKV_HEAD_DIM
LORA_DECAY = 64
LORA_AAA = 64
LORA_GATE = 128
RWKV_IN = 3 * RWKV_WIDTH + 2 * LORA_DECAY + 2 * LORA_AAA + LORA_GATE
N_GROUPS = 4
EXPERTS_PER_GROUP = 8
N_EXPERTS = N_GROUPS * EXPERTS_PER_GROUP
TOP_K = 2
EXPERT_HIDDEN = 512
DEPTH = 1
ALPHA = (2.0 * DEPTH) ** 0.25
LN_EPS = 1e-5
RMS_EPS = 1e-5
RWKV_GN_EPS = 64e-5
LAM_INIT = 0.8 - 0.6 * math.exp(-0.3 * 0)

LANES = 128
ROUTE_LANES = 128
VMEM_LIMIT = 56 * 1024 * 1024

PROJ_TILE = 512
ATT_QB = 128
ATT_KB = 512
ATT_STEP_BLOCKS = 4
PRE_TILE = 256
CHUNK = 128
MIX_TILE = 256
OUT_TILE = 256
EXPERT_ROWS = 512
NEG_BIG = -1e30
LOG2E = math.log2(math.e)
Q_SCALE = ATT_QK_DIM ** -0.5 * LOG2E


def _bdot(a, b):
    return jnp.dot(a.astype(BF16), b.astype(BF16), preferred_element_type=F32)


def _bdot_nt(a, b):
    return lax.dot_general(a.astype(BF16), b.astype(BF16), (((1,), (1,)), ((), ())),
                           preferred_element_type=F32)


def _split3(x):
    hi = x.astype(BF16)
    r1 = x - hi.astype(F32)
    mid = r1.astype(BF16)
    lo = (r1 - mid.astype(F32)).astype(BF16)
    return hi, mid, lo


def _params(sem):
    return pltpu.CompilerParams(dimension_semantics=sem, vmem_limit_bytes=VMEM_LIMIT)


def _in_proj_kernel(x_ref, wqk_ref, wvt_ref, wrw_ref, qk_ref, vt_ref, rw_ref):
    xb = x_ref[0].astype(BF16)
    qk = jnp.dot(xb, wqk_ref[...], preferred_element_type=F32)
    qk_ref[0, :, :QK_WIDTH // 2] = (qk[:, :QK_WIDTH // 2] * Q_SCALE).astype(BF16)
    qk_ref[0, :, QK_WIDTH // 2:] = qk[:, QK_WIDTH // 2:].astype(BF16)
    vt_ref[0] = lax.dot_general(wvt_ref[...], xb, (((1,), (1,)), ((), ())),
                                preferred_element_type=F32).astype(BF16)
    rw_ref[0] = jnp.dot(xb, wrw_ref[...], preferred_element_type=F32)


def _in_proj(x, wqk, wvt, wrw):
    B, S, D = x.shape
    T = PROJ_TILE
    return pl.pallas_call(
        _in_proj_kernel,
        grid=(B, S // T),
        in_specs=[
            pl.BlockSpec((1, T, D), lambda b, i: (b, i, 0)),
            pl.BlockSpec((D, QK_WIDTH), lambda b, i: (0, 0)),
            pl.BlockSpec((ATT_WIDTH, D), lambda b, i: (0, 0)),
            pl.BlockSpec((D, RWKV_IN), lambda b, i: (0, 0)),
        ],
        out_specs=[
            pl.BlockSpec((1, T, QK_WIDTH), lambda b, i: (b, i, 0)),
            pl.BlockSpec((1, ATT_WIDTH, T), lambda b, i: (b, 0, i)),
            pl.BlockSpec((1, T, RWKV_IN), lambda b, i: (b, i, 0)),
        ],
        out_shape=[
            jax.ShapeDtypeStruct((B, S, QK_WIDTH), BF16),
            jax.ShapeDtypeStruct((B, ATT_WIDTH, S), BF16),
            jax.ShapeDtypeStruct((B, S, RWKV_IN), F32),
        ],
        compiler_params=_params(("parallel", "parallel")),
        name="in_proj",
    )(x, wqk, wvt, wrw)


def _stack_components(q):
    lane = lax.broadcasted_iota(jnp.int32, q.shape, 1)
    zero = jnp.zeros_like(q)
    return jnp.concatenate([jnp.where(lane < ATT_QK_DIM, q, zero),
                            jnp.where(lane >= ATT_QK_DIM, q, zero)], axis=0)


def _score_tile(k_ref, bias_ref, qq, win, j):
    KB = ATT_KB
    kt = k_ref[0, j * KB:(j + 1) * KB, :]
    s = lax.dot_general(kt, qq, (((1,), (1,)), ((), ())), preferred_element_type=F32)
    bias = bias_ref[pl.ds(win + j * KB, KB), :]
    return s - jnp.concatenate([bias, bias], axis=1)


def _scores_and_values(k_ref, vt_ref, bias_ref, qq_new, win_new, s_new, s_old, m_old, seq):
    QB, KB = ATT_QB, ATT_KB
    m_new = jnp.full((1, 2 * QB), NEG_BIG, F32)
    l = jnp.zeros((1, 2 * QB), F32)
    acc = jnp.zeros((ATT_V_DIM, 2 * QB), F32)
    for j in range(seq // KB):
        p = jnp.exp2(s_old[j * KB:(j + 1) * KB, :] - m_old)
        s = _score_tile(k_ref, bias_ref, qq_new, win_new, j)
        s_new[j * KB:(j + 1) * KB, :] = s
        m_new = jnp.maximum(m_new, jnp.max(s, axis=0, keepdims=True))
        l = l + jnp.sum(p, axis=0, keepdims=True)
        vt = vt_ref[0, :, j * KB:(j + 1) * KB]
        acc = acc + jnp.dot(vt, p.astype(BF16), preferred_element_type=F32)
    return m_new, l, acc


def _attention_kernel(slopes_ref, q_ref, qn_ref, k_ref, vt_ref, lam_ref, g_ref, o_ref,
                      bias_ref, sa_ref, sb_ref, ma_ref, *, seq):
    h = pl.program_id(1)
    i = pl.program_id(2)
    QB, KB = ATT_QB, ATT_KB
    nq = seq // QB
    slope = slopes_ref[h]

    @pl.when(i == 0)
    def _():
        for c in range(2 * seq // KB):
            r = lax.broadcasted_iota(jnp.int32, (KB, QB), 0) + (c * KB - seq)
            qq_ = lax.broadcasted_iota(jnp.int32, (KB, QB), 1)
            bias_ref[c * KB:(c + 1) * KB, :] = slope * jnp.abs(r - qq_).astype(F32)
        qq0 = _stack_components(q_ref[0, :QB, :])
        m0 = jnp.full((1, 2 * QB), NEG_BIG, F32)
        for j in range(seq // KB):
            s = _score_tile(k_ref, bias_ref, qq0, seq, j)
            sa_ref[j * KB:(j + 1) * KB, :] = s
            m0 = jnp.maximum(m0, jnp.max(s, axis=0, keepdims=True))
        ma_ref[...] = m0

    lmb = lam_ref[...]
    lam = (jnp.exp(jnp.sum(lmb[0:1] * lmb[1:2], axis=1, keepdims=True))
           - jnp.exp(jnp.sum(lmb[2:3] * lmb[3:4], axis=1, keepdims=True)) + LAM_INIT)

    def finish(l, acc):
        o = acc[:, :QB] / l[:, :QB] - lam * (acc[:, QB:] / l[:, QB:])
        ms = jnp.mean(o * o, axis=0, keepdims=True)
        o = o * lax.rsqrt(ms + RMS_EPS) * g_ref[...] * (1.0 - LAM_INIT)
        return o.T.astype(BF16)

    NB = ATT_STEP_BLOCKS
    bufs = (sa_ref, sb_ref)
    m_old = ma_ref[...]
    for p in range(NB):
        q_new = q_ref[0, (p + 1) * QB:(p + 2) * QB, :] if p + 1 < NB else qn_ref[0]
        win = pl.multiple_of(seq - jnp.minimum(NB * i + p + 1, nq - 1) * QB, QB)
        m_old, l, acc = _scores_and_values(k_ref, vt_ref, bias_ref, _stack_components(q_new), win,
                                           bufs[(p + 1) % 2], bufs[p % 2], m_old, seq)
        o_ref[0, p * QB:(p + 1) * QB, :] = finish(l, acc)
    ma_ref[...] = m_old


def _attention(qk, vt, att_lambda, subln_g):
    B, S, _ = qk.shape
    H = ATT_HEADS
    nq = S // ATT_QB
    NB = ATT_STEP_BLOCKS
    slopes = jnp.asarray([LOG2E * 2.0 ** (-8.0 * (i + 1) / H) for i in range(H)], F32)
    return pl.pallas_call(
        functools.partial(_attention_kernel, seq=S),
        grid_spec=pltpu.PrefetchScalarGridSpec(
            num_scalar_prefetch=1,
            grid=(B, H, nq // NB),
            in_specs=[
                pl.BlockSpec((1, NB * ATT_QB, LANES), lambda b, h, i, s: (b, i, h)),
                pl.BlockSpec((1, ATT_QB, LANES),
                             lambda b, h, i, s: (b, jnp.minimum(NB * i + NB, nq - 1), h)),
                pl.BlockSpec((1, S, LANES), lambda b, h, i, s: (b, 0, H + h)),
                pl.BlockSpec((1, ATT_V_DIM, S), lambda b, h, i, s: (b, h, 0)),
                pl.BlockSpec((4, ATT_QK_DIM), lambda b, h, i, s: (0, 0)),
                pl.BlockSpec((ATT_V_DIM, 1), lambda b, h, i, s: (0, 0)),
            ],
            out_specs=pl.BlockSpec((1, NB * ATT_QB, LANES), lambda b, h, i, s: (b, i, h)),
            scratch_shapes=[pltpu.VMEM((2 * S, ATT_QB), F32),
                            pltpu.VMEM((S, 2 * ATT_QB), F32),
                            pltpu.VMEM((S, 2 * ATT_QB), F32),
                            pltpu.VMEM((1, 2 * ATT_QB), F32)],
        ),
        out_shape=jax.ShapeDtypeStruct((B, S, ATT_WIDTH), BF16),
        compiler_params=_params(("parallel", "parallel", "arbitrary")),
        name="attention",
    )(slopes, qk, qk, qk, vt, att_lambda, subln_g.reshape(ATT_V_DIM, 1))


def _head_sum(x, blk):
    hi, mid, lo = _split3(x)
    return (jnp.dot(hi, blk, preferred_element_type=F32) + jnp.dot(mid, blk, preferred_element_type=F32)
            + jnp.dot(lo, blk, preferred_element_type=F32))


def _rwkv_pre_kernel(p_ref, prev_ref, next_ref, mu_ref, w0_ref, wup_ref, a0_ref, aup_ref, gup_ref,
                     kk_ref, ka_ref, rk_ref, blk_ref,
                     r_ref, a_ref, lw_ref, kd_ref, kka_ref, vt_ref, g_ref, bonus_ref):
    i = pl.program_id(1)
    n = pl.num_programs(1)
    T = PRE_TILE
    W = RWKV_WIDTH
    p = p_ref[0]
    row = lax.broadcasted_iota(jnp.int32, p.shape, 0)
    prev_row = jnp.where(i > 0, prev_ref[0, 7:8, :], 0.0)
    next_row = jnp.where(i < n - 1, next_ref[0, 0:1, :], 0.0)
    prev = jnp.where(row == 0, prev_row, pltpu.roll(p, 1, axis=0))
    nxt = jnp.where(row == T - 1, next_row, pltpu.roll(p, T - 1, axis=0))
    p = p + mu_ref[0:1, :] * (prev - p) + mu_ref[1:2, :] * (nxt - p)

    r = p[:, 0:W]
    k = p[:, W:2 * W]
    v = p[:, 2 * W:3 * W]
    c = 3 * W
    wd = p[:, c:c + 2 * LORA_DECAY]
    ad = p[:, c + 2 * LORA_DECAY:c + 2 * LORA_DECAY + 2 * LORA_AAA]
    gd = p[:, c + 2 * LORA_DECAY + 2 * LORA_AAA:]

    wl = w0_ref[...] + _bdot(jnp.tanh(wd), wup_ref[...])
    lw = -math.exp(-0.5) * jax.nn.sigmoid(wl)
    av = jax.nn.sigmoid(a0_ref[...] + _bdot(ad, aup_ref[...]))
    g_ref[0] = _bdot(jax.nn.sigmoid(gd), gup_ref[...])

    blk = blk_ref[...]
    kkr = k * kk_ref[...]
    kk = kkr / jnp.maximum(jnp.sqrt(_head_sum(kkr * kkr, blk)), 1e-12)
    ksum = jnp.zeros_like(k)
    for d in range(2):
        a_d = av[:, d * W:(d + 1) * W]
        k_d = k * (1.0 + (a_d - 1.0) * ka_ref[...])
        ksum = ksum + k_d
        lw_ref[d, 0] = lw[:, d * W:(d + 1) * W]
        kd_ref[d, 0] = k_d
        kka_ref[d, 0] = kk * a_d
    r_ref[0] = r
    a_ref[0] = -kk
    vt_ref[0] = v.T
    bonus_ref[0] = _head_sum(r * ksum * rk_ref[...], blk) * v


def _rwkv_pre(rw, mu, w0cat, wup_blk, a0cat, aup_blk, gup, k_k, k_a, r_k, blk):
    B, S, _ = rw.shape
    T = PRE_TILE
    W = RWKV_WIDTH
    nb8 = S // 8
    full = lambda shape: pl.BlockSpec(shape, lambda b, i: (0,) * len(shape))
    row_spec = pl.BlockSpec((1, T, W), lambda b, i: (b, i, 0))
    dir_spec = pl.BlockSpec((2, 1, T, W), lambda b, i: (0, b, i, 0))
    row_shape = jax.ShapeDtypeStruct((B, S, W), F32)
    dir_shape = jax.ShapeDtypeStruct((2, B, S, W), F32)
    return pl.pallas_call(
        _rwkv_pre_kernel,
        grid=(B, S // T),
        in_specs=[
            pl.BlockSpec((1, T, RWKV_IN), lambda b, i: (b, i, 0)),
            pl.BlockSpec((1, 8, RWKV_IN), lambda b, i: (b, jnp.maximum(i * (T // 8) - 1, 0), 0)),
            pl.BlockSpec((1, 8, RWKV_IN), lambda b, i: (b, jnp.minimum((i + 1) * (T // 8), nb8 - 1), 0)),
            full((2, RWKV_IN)), full((1, 2 * W)), full((2 * LORA_DECAY, 2 * W)),
            full((1, 2 * W)), full((2 * LORA_AAA, 2 * W)), full((LORA_GATE, W)),
            full((1, W)), full((1, W)), full((1, W)), full((W, W)),
        ],
        out_specs=[row_spec, row_spec, dir_spec, dir_spec, dir_spec,
                   pl.BlockSpec((1, W, T), lambda b, i: (b, 0, i)), row_spec, row_spec],
        out_shape=[row_shape, row_shape, dir_shape, dir_shape, dir_shape,
                   jax.ShapeDtypeStruct((B, W, S), F32), row_shape, row_shape],
        compiler_params=_params(("parallel", "parallel")),
        name="rwkv_pre",
    )(rw, rw, rw, mu, w0cat, wup_blk, a0cat, aup_blk, gup, k_k, k_a, r_k, blk)


def _rwkv_scan_kernel(r_ref, a_ref, lw_ref, kd_ref, kka_ref, vt_ref, yt_ref, state_ref):
    d = pl.program_id(1)
    c = pl.program_id(2)
    C = CHUNK
    N = RWKV_HEAD_DIM

    @pl.when(c == 0)
    def _():
        state_ref[...] = jnp.zeros_like(state_ref)

    row = lax.broadcasted_iota(jnp.int32, (C, C), 0)
    col = lax.broadcasted_iota(jnp.int32, (C, C), 1)
    order = (row - col) * (1 - 2 * d)
    strict = order > 0
    incl = order >= 0
    tri = jnp.where(incl, 1.0, 0.0).astype(BF16)
    eye = jnp.where(row == col, 1.0, 0.0)

    lw = lw_ref[0, 0]
    k = kd_ref[0, 0]
    b = kka_ref[0, 0]
    lw_hi, lw_mid, lw_lo = _split3(lw)
    cs = (jnp.dot(tri, lw_hi, preferred_element_type=F32) + jnp.dot(tri, lw_mid, preferred_element_type=F32)
          + jnp.dot(tri, lw_lo, preferred_element_type=F32))
    tot = jnp.sum(lw, axis=0, keepdims=True)
    g_inv = jnp.exp(-cs)
    g_last = jnp.exp(tot - cs)
    g_tot = jnp.exp(tot)
    a_t = (a_ref[0] * jnp.exp(cs - lw)).astype(BF16)
    r_t = (r_ref[0] * jnp.exp(cs)).astype(BF16)
    lhs = jnp.concatenate([a_t, r_t], axis=0)
    rhs = jnp.concatenate([(b * g_inv).astype(BF16), (k * g_inv).astype(BF16)], axis=0)
    b_end = (b * g_last).astype(BF16)
    k_end = (k * g_last).astype(BF16)

    heads = range(RWKV_HEADS)
    sl = [slice(h * N, (h + 1) * N) for h in heads]
    vt = [vt_ref[0, sl[h], :].astype(BF16) for h in heads]
    s0 = [state_ref[h] for h in heads]
    gram = [_bdot_nt(lhs[:, sl[h]], rhs[:, sl[h]]) for h in heads]
    l_ab = [jnp.where(strict, gram[h][:C, :C], 0.0) for h in heads]
    l_ak = [jnp.where(strict, gram[h][:C, C:], 0.0).astype(BF16) for h in heads]
    m_rb = [jnp.where(incl, gram[h][C:, :C], 0.0).astype(BF16) for h in heads]
    m_rk = [jnp.where(incl, gram[h][C:, C:], 0.0).astype(BF16) for h in heads]

    levels = int(math.log2(C))
    t = [eye + l_ab[h] for h in heads]
    pw = [_bdot(l_ab[h], l_ab[h]) for h in heads]
    for j in range(1, levels):
        pb = [pw[h].astype(BF16) for h in heads]
        if j < levels - 1:
            both = [_bdot(pb[h], jnp.concatenate([pb[h], t[h].astype(BF16)], axis=1)) for h in heads]
            pw = [both[h][:, :C] for h in heads]
            t = [t[h] + both[h][:, C:] for h in heads]
        else:
            t = [t[h] + _bdot(pb[h], t[h]) for h in heads]

    tb = [t[h].astype(BF16) for h in heads]
    xw = [_bdot(tb[h], jnp.concatenate([l_ak[h], a_t[:, sl[h]]], axis=1)) for h in heads]
    s0b = [s0[h].astype(BF16) for h in heads]
    ut = [_bdot_nt(jnp.concatenate([vt[h], s0b[h]], axis=1), xw[h]).astype(BF16)
          for h in heads]
    for h in heads:
        yt_ref[0, 0, sl[h], :] = _bdot_nt(
            jnp.concatenate([ut[h], vt[h], s0b[h]], axis=1),
            jnp.concatenate([m_rb[h], m_rk[h], r_t[:, sl[h]]], axis=1))
    for h in heads:
        state_ref[h] = s0[h] * g_tot[:, sl[h]] + _bdot(
            jnp.concatenate([ut[h], vt[h]], axis=1),
            jnp.concatenate([b_end[:, sl[h]], k_end[:, sl[h]]], axis=0))


def _rwkv_scan(r, a, lw, kd, kka, vt):
    B, S, W = r.shape
    C = CHUNK
    nc = S // C
    cidx = lambda d, c: c + d * (nc - 1 - 2 * c)
    row_spec = pl.BlockSpec((1, C, W), lambda b, d, c: (b, cidx(d, c), 0))
    dir_spec = pl.BlockSpec((1, 1, C, W), lambda b, d, c: (d, b, cidx(d, c), 0))
    return pl.pallas_call(
        _rwkv_scan_kernel,
        grid=(B, 2, nc),
        in_specs=[row_spec, row_spec, dir_spec, dir_spec, dir_spec,
                  pl.BlockSpec((1, W, C), lambda b, d, c: (b, 0, cidx(d, c)))],
        out_specs=pl.BlockSpec((1, 1, W, C), lambda b, d, c: (d, b, 0, cidx(d, c))),
        out_shape=jax.ShapeDtypeStruct((2, B, W, S), F32),
        scratch_shapes=[pltpu.VMEM((RWKV_HEADS, RWKV_HEAD_DIM, RWKV_HEAD_DIM), F32)],
        compiler_params=_params(("parallel", "parallel", "arbitrary")),
        name="rwkv_scan",
    )(r, a, lw, kd, kka, vt)


def _layernorm(z, g, b):
    mu = jnp.mean(z, axis=-1, keepdims=True)
    zc = z - mu
    var = jnp.mean(zc * zc, axis=-1, keepdims=True)
    return zc * lax.rsqrt(var + LN_EPS) * g + b


def _first_lane_where(mask, lane):
    return jnp.min(jnp.where(mask, lane, ROUTE_LANES), axis=1, keepdims=True)


def _mix_ln1_kernel(x_ref, att_ref, ytf_ref, ytb_ref, g_ref, bonus_ref, gng_ref, gnb_ref,
                    woa_ref, wob_ref, l1g_ref, l1b_ref, wrh_ref, wrl_ref, br_ref,
                    h_ref, exp_ref, gate_ref, cnt_ref, base_ref):
    T = MIX_TILE
    N = RWKV_HEAD_DIM
    yt = ytf_ref[0, 0] + ytb_ref[0, 0]
    y3 = yt.reshape(RWKV_HEADS, N, T)
    mu = jnp.mean(y3, axis=1, keepdims=True)
    yc = y3 - mu
    var = jnp.mean(yc * yc, axis=1, keepdims=True)
    yn = (yc * lax.rsqrt(var + RWKV_GN_EPS)).reshape(RWKV_WIDTH, T)
    yn = yn * gng_ref[...] + gnb_ref[...]
    tm = (yn.T + bonus_ref[0]) * g_ref[0]
    mix = (jnp.dot(att_ref[0], woa_ref[...], preferred_element_type=F32)
           + _bdot(tm, wob_ref[...]))
    h = _layernorm(ALPHA * x_ref[0] + mix, l1g_ref[...], l1b_ref[...])
    h_ref[0] = h

    h_hi = h.astype(BF16)
    h_lo = (h - h_hi.astype(F32)).astype(BF16)
    logits = (jnp.dot(h_hi, wrh_ref[...], preferred_element_type=F32)
              + jnp.dot(h_lo, wrh_ref[...], preferred_element_type=F32)
              + jnp.dot(h_hi, wrl_ref[...], preferred_element_type=F32)) + br_ref[...]
    lane = lax.broadcasted_iota(jnp.int32, (T, ROUTE_LANES), 1)
    neg = jnp.asarray(-jnp.inf, F32)
    gl = jnp.where(lane < N_GROUPS, logits, neg)
    gmax = jnp.max(gl, axis=1, keepdims=True)
    grp = _first_lane_where(gl == gmax, lane)
    g1 = 1.0 / jnp.sum(jnp.exp(gl - gmax), axis=1, keepdims=True)
    lo = N_GROUPS + EXPERTS_PER_GROUP * grp
    sel = jnp.where((lane >= lo) & (lane < lo + EXPERTS_PER_GROUP), logits, neg)
    v1 = jnp.max(sel, axis=1, keepdims=True)
    i1 = _first_lane_where(sel == v1, lane)
    sel2 = jnp.where(lane == i1, neg, sel)
    v2 = jnp.max(sel2, axis=1, keepdims=True)
    i2 = _first_lane_where(sel2 == v2, lane)
    e2 = jnp.exp(v2 - v1)
    den = 1.0 + e2
    gate1 = g1 * (1.0 / den)
    gate2 = g1 * (e2 / den)
    gate_ref[0] = jnp.where(lane == 0, gate1, jnp.where(lane == 1, gate2, 0.0))

    @pl.when((pl.program_id(0) == 0) & (pl.program_id(1) == 0))
    def _():
        base_ref[...] = jnp.zeros_like(base_ref)

    e1 = i1 - N_GROUPS
    e2 = i2 - N_GROUPS
    oh1 = jnp.where(lane == e1, 1.0, 0.0)
    oh2 = jnp.where(lane == e2, 1.0, 0.0)
    cnt = oh1 + oh2
    ti = lax.broadcasted_iota(jnp.int32, (T, T), 0)
    tj = lax.broadcasted_iota(jnp.int32, (T, T), 1)
    earlier = jnp.where(tj < ti, 1.0, 0.0).astype(BF16)
    before = jnp.dot(earlier, cnt.astype(BF16), preferred_element_type=F32) + base_ref[...]
    rank1 = jnp.sum(oh1 * before, axis=1, keepdims=True).astype(jnp.int32)
    rank2 = jnp.sum(oh2 * before, axis=1, keepdims=True).astype(jnp.int32)
    total = base_ref[...] + jnp.sum(cnt, axis=0, keepdims=True)
    base_ref[...] = total
    cnt_ref[...] = total
    exp_ref[0] = jnp.where(lane == 0, e1, jnp.where(lane == 1, e2, jnp.where(
        lane == 2, rank1, jnp.where(lane == 3, rank2, 0))))


def _mix_ln1(x, att, yt, g, bonus, gn_g, gn_b, woa, wob, l1g, l1b, wr_hi, wr_lo, br):
    B, S, D = x.shape
    T = MIX_TILE
    W = RWKV_WIDTH
    full = lambda shape: pl.BlockSpec(shape, lambda b, i: (0,) * len(shape))
    row = lambda width: pl.BlockSpec((1, T, width), lambda b, i: (b, i, 0))
    return pl.pallas_call(
        _mix_ln1_kernel,
        grid=(B, S // T),
        in_specs=[
            row(D), row(ATT_WIDTH),
            pl.BlockSpec((1, 1, W, T), lambda b, i: (0, b, 0, i)),
            pl.BlockSpec((1, 1, W, T), lambda b, i: (1, b, 0, i)),
            row(W), row(W), full((W, 1)), full((W, 1)),
            full((ATT_WIDTH, D)), full((W, D)), full((1, D)), full((1, D)),
            full((D, ROUTE_LANES)), full((D, ROUTE_LANES)), full((1, ROUTE_LANES)),
        ],
        out_specs=[row(D), row(ROUTE_LANES), row(ROUTE_LANES), full((1, ROUTE_LANES))],
        out_shape=[jax.ShapeDtypeStruct((B, S, D), F32),
                   jax.ShapeDtypeStruct((B, S, ROUTE_LANES), jnp.int32),
                   jax.ShapeDtypeStruct((B, S, ROUTE_LANES), F32),
                   jax.ShapeDtypeStruct((1, ROUTE_LANES), F32)],
        scratch_shapes=[pltpu.VMEM((1, ROUTE_LANES), F32)],
        compiler_params=_params(("arbitrary", "arbitrary")),
        name="mix_ln1",
    )(x, att, yt, yt, g, bonus, gn_g, gn_b, woa, wob, l1g, l1b, wr_hi, wr_lo, br)


def _start_row_gather(idx_ref, src_hbm, dst, sem, rows):
    for r in range(rows):
        pltpu.make_async_copy(src_hbm.at[pl.ds(idx_ref[0, 0, r], 1), :],
                              dst.at[pl.ds(r, 1), :], sem).start()


def _wait_row_gather(src_hbm, dst, sem, rows):
    pltpu.make_async_copy(src_hbm.at[pl.ds(0, rows), :], dst, sem).wait()


def _dispatch_kernel(pos_ref, h_ref, xs_in_hbm, xs_hbm, sem):
    del xs_in_hbm
    T = OUT_TILE
    for k in range(TOP_K):
        for r in range(T):
            pltpu.make_async_copy(h_ref.at[pl.ds(r, 1), :],
                                  xs_hbm.at[pl.ds(pos_ref[0, 0, k * T + r], 1), :], sem).start()
    for k in range(TOP_K):
        pltpu.make_async_copy(h_ref, xs_hbm.at[pl.ds(0, T), :], sem).wait()


def _dispatch(h_flat, pos_tiles, n_rows):
    n_tok, D = h_flat.shape
    T = OUT_TILE
    nt = n_tok // T
    return pl.pallas_call(
        _dispatch_kernel,
        grid=(nt,),
        in_specs=[
            pl.BlockSpec((1, 1, TOP_K * T), lambda i: (i, 0, 0), memory_space=pltpu.SMEM),
            pl.BlockSpec((T, D), lambda i: (i, 0)),
            pl.BlockSpec(memory_space=pl.ANY),
        ],
        out_specs=pl.BlockSpec(memory_space=pl.ANY),
        out_shape=jax.ShapeDtypeStruct((n_rows, D), F32),
        scratch_shapes=[pltpu.SemaphoreType.DMA(())],
        input_output_aliases={2: 0},
        compiler_params=_params(("arbitrary",)),
        name="dispatch",
    )(pos_tiles, h_flat, jnp.zeros((n_rows, D), F32))


def _experts_kernel(bexp_ref, x_ref, wg_ref, wu_ref, wd_ref, y_ref):
    del bexp_ref
    xb = x_ref[...].astype(BF16)
    hid = (jax.nn.silu(jnp.dot(xb, wg_ref[0], preferred_element_type=F32))
           * jnp.dot(xb, wu_ref[0], preferred_element_type=F32))
    y_ref[...] = _bdot(hid, wd_ref[0])


def _experts(xs, block_exp, wg, wu, wd):
    D = D_MODEL
    R = EXPERT_ROWS
    nb = block_exp.shape[0]
    return pl.pallas_call(
        _experts_kernel,
        grid_spec=pltpu.PrefetchScalarGridSpec(
            num_scalar_prefetch=1,
            grid=(nb,),
            in_specs=[
                pl.BlockSpec((R, D), lambda i, e: (i, 0)),
                pl.BlockSpec((1, D, EXPERT_HIDDEN), lambda i, e: (e[i], 0, 0)),
                pl.BlockSpec((1, D, EXPERT_HIDDEN), lambda i, e: (e[i], 0, 0)),
                pl.BlockSpec((1, EXPERT_HIDDEN, D), lambda i, e: (e[i], 0, 0)),
            ],
            out_specs=pl.BlockSpec((R, D), lambda i, e: (i, 0)),
        ),
        out_shape=jax.ShapeDtypeStruct((nb * R, D), F32),
        compiler_params=_params(("parallel",)),
        name="experts",
    )(block_exp, xs, wg, wu, wd)


def _combine_ln2_kernel(pos_ref, pos_next_ref, h_ref, gate_ref, y_hbm, l2g_ref, l2b_ref, o_ref,
                        ybuf, sem):
    i = pl.program_id(0)
    n = pl.num_programs(0)
    T = OUT_TILE
    R = TOP_K * T
    slot = i % 2

    @pl.when(i == 0)
    def _():
        _start_row_gather(pos_ref, y_hbm, ybuf.at[0], sem.at[0], R)

    @pl.when(i + 1 < n)
    def _():
        _start_row_gather(pos_next_ref, y_hbm, ybuf.at[1 - slot], sem.at[1 - slot], R)

    _wait_row_gather(y_hbm, ybuf.at[slot], sem.at[slot], R)
    gates = gate_ref[...]
    moe = ybuf[slot, 0:T, :] * gates[:, 0:1] + ybuf[slot, T:R, :] * gates[:, 1:2]
    o_ref[...] = _layernorm(ALPHA * h_ref[...] + moe, l2g_ref[...], l2b_ref[...])


def _combine_ln2(h_flat, gates, pos, yb, l2g, l2b):
    n_tok, D = h_flat.shape
    T = OUT_TILE
    nt = n_tok // T
    idx_spec = lambda f: pl.BlockSpec((1, 1, TOP_K * T), f, memory_space=pltpu.SMEM)
    return pl.pallas_call(
        _combine_ln2_kernel,
        grid=(nt,),
        in_specs=[
            idx_spec(lambda i: (i, 0, 0)),
            idx_spec(lambda i: (jnp.minimum(i + 1, nt - 1), 0, 0)),
            pl.BlockSpec((T, D), lambda i: (i, 0)),
            pl.BlockSpec((T, ROUTE_LANES), lambda i: (i, 0)),
            pl.BlockSpec(memory_space=pl.ANY),
            pl.BlockSpec((1, D), lambda i: (0, 0)),
            pl.BlockSpec((1, D), lambda i: (0, 0)),
        ],
        out_specs=pl.BlockSpec((T, D), lambda i: (i, 0)),
        out_shape=jax.ShapeDtypeStruct((n_tok, D), F32),
        scratch_shapes=[pltpu.VMEM((2, TOP_K * T, D), F32), pltpu.SemaphoreType.DMA((2,))],
        compiler_params=_params(("arbitrary",)),
        name="combine_ln2",
    )(pos, pos, h_flat, gates, yb, l2g, l2b)


def _dispatch_plan(expert, rank, counts):
    n_tok = expert.shape[0]
    R = EXPERT_ROWS
    padded = (counts + R - 1) // R * R
    pends = jnp.cumsum(padded)
    pstarts = pends - padded
    ids = jnp.arange(N_EXPERTS, dtype=jnp.int32)
    pos = rank + jnp.sum(jnp.where(expert[..., None] == ids, pstarts, 0), axis=-1)
    nb = n_tok * TOP_K // R + N_EXPERTS
    block_start = jnp.arange(nb, dtype=jnp.int32) * R
    block_exp = jnp.minimum(jnp.sum(block_start[:, None] >= pends[None, :], axis=1),
                            N_EXPERTS - 1).astype(jnp.int32)
    return pos.astype(jnp.int32), block_exp


def _trunk(x, w):
    B, S, D = x.shape
    qk, vt, rw = _in_proj(x, w["wqk"], w["wvt"], w["wrw"])
    att = _attention(qk, vt, w["att_lambda"], w["subln_g"])
    r, a, lw, kd, kka, rvt, g, bonus = _rwkv_pre(
        rw, w["mu"], w["w0cat"], w["wup_blk"], w["a0cat"], w["aup_blk"], w["gup"],
        w["k_k"], w["k_a"], w["r_k"], w["blk"])
    yt = _rwkv_scan(r, a, lw, kd, kka, rvt)
    h, route, gates, counts = _mix_ln1(x, att, yt, g, bonus, w["gn_g"], w["gn_b"], w["woa"], w["wob"],
                                       w["l1g"], w["l1b"], w["wr_hi"], w["wr_lo"], w["br"])
    n_tok = B * S
    h_flat = h.reshape(n_tok, D)
    route = route.reshape(n_tok, ROUTE_LANES)
    pos, block_exp = _dispatch_plan(route[:, :TOP_K], route[:, TOP_K:2 * TOP_K],
                                    counts[0, :N_EXPERTS].astype(jnp.int32))
    T = OUT_TILE
    pos_tiles = pos.reshape(n_tok // T, T, TOP_K).transpose(0, 2, 1).reshape(n_tok // T, 1, TOP_K * T)
    xs = _dispatch(h_flat, pos_tiles, block_exp.shape[0] * EXPERT_ROWS)
    yb = _experts(xs, block_exp, w["wg"], w["wu"], w["wd"])
    out = _combine_ln2(h_flat, gates.reshape(n_tok, ROUTE_LANES), pos_tiles, yb, w["l2g"], w["l2b"])
    return out.reshape(B, S, D)


def _block_diag2(m):
    z = jnp.zeros_like(m[0])
    return jnp.concatenate([jnp.concatenate([m[0], z], axis=1),
                            jnp.concatenate([z, m[1]], axis=1)], axis=0)


def kernel(x_prompt, x_sample, w_in, att_lambda, att_subln_g, rwkv_mu, rwkv_w0, rwkv_w_up, rwkv_a0, rwkv_a_up, rwkv_g_up, rwkv_k_k, rwkv_k_a, rwkv_r_k, rwkv_ln_g, rwkv_ln_b, w_out, ln1_g, ln1_b, router_g_w, router_g_b, router_e_w, router_e_b, exp_w_gate, exp_w_up, exp_w_down, ln2_g, ln2_b):
    assert w_in.shape[0] == DEPTH
    W = RWKV_WIDTH
    D = D_MODEL
    win = w_in[0]
    head = jnp.arange(W, dtype=jnp.int32) // RWKV_HEAD_DIM
    n_route = N_GROUPS + N_EXPERTS
    wr = jnp.concatenate([router_g_w[0],
                          jnp.transpose(router_e_w[0], (1, 0, 2)).reshape(D, N_EXPERTS)], axis=1)
    br = jnp.concatenate([router_g_b[0], router_e_b[0].reshape(N_EXPERTS)])
    wr_pad = jnp.pad(wr, ((0, 0), (0, ROUTE_LANES - n_route)))
    wr_hi = wr_pad.astype(BF16)
    w = {
        "wqk": win[:, :QK_WIDTH].astype(BF16),
        "wvt": win[:, QK_WIDTH:QK_WIDTH + ATT_WIDTH].T.astype(BF16),
        "wrw": win[:, QK_WIDTH + ATT_WIDTH:].astype(BF16),
        "att_lambda": att_lambda[0],
        "subln_g": att_subln_g[0],
        "mu": rwkv_mu[0],
        "w0cat": rwkv_w0[0].reshape(1, 2 * W),
        "wup_blk": _block_diag2(rwkv_w_up[0]).astype(BF16),
        "a0cat": rwkv_a0[0].reshape(1, 2 * W),
        "aup_blk": _block_diag2(rwkv_a_up[0]).astype(BF16),
        "gup": rwkv_g_up[0].astype(BF16),
        "k_k": rwkv_k_k[0].reshape(1, W),
        "k_a": rwkv_k_a[0].reshape(1, W),
        "r_k": rwkv_r_k[0].reshape(1, W),
        "blk": (head[:, None] == head[None, :]).astype(BF16),
        "gn_g": rwkv_ln_g[0].reshape(W, 1),
        "gn_b": rwkv_ln_b[0].reshape(W, 1),
        "woa": w_out[0][:ATT_WIDTH].astype(BF16),
        "wob": w_out[0][ATT_WIDTH:].astype(BF16),
        "l1g": ln1_g[0].reshape(1, D),
        "l1b": ln1_b[0].reshape(1, D),
        "wr_hi": wr_hi,
        "wr_lo": (wr_pad - wr_hi.astype(F32)).astype(BF16),
        "br": jnp.pad(br, (0, ROUTE_LANES - n_route)).reshape(1, ROUTE_LANES),
        "wg": exp_w_gate[0].astype(BF16),
        "wu": exp_w_up[0].astype(BF16),
        "wd": exp_w_down[0].astype(BF16),
        "l2g": ln2_g[0].reshape(1, D),
        "l2b": ln2_b[0].reshape(1, D),
    }
    return (_trunk(x_prompt, w), _trunk(x_sample, w))
```

```python
import functools
import math

import jax
import jax.numpy as jnp
from jax import lax
from jax.experimental import pallas as pl
from jax.experimental.pallas import tpu as pltpu

F32 = jnp.float32
BF16 = jnp.bfloat16

D_MODEL = 1024
ATT_HEADS = 4
ATT_QK_DIM = 64
ATT_V_DIM = 128
ATT_WIDTH = ATT_HEADS * ATT_V_DIM
QK_WIDTH = 2 * ATT_HEADS * 2 * ATT_QK_DIM
RWKV_HEADS = 8
RWKV_HEAD_DIM = 64
RWKV_WIDTH = RWKV_HEADS * RWKV_HEAD_DIM
LORA_DECAY = 64
LORA_AAA = 64
LORA_GATE = 128
RWKV_IN = 3 * RWKV_WIDTH + 2 * LORA_DECAY + 2 * LORA_AAA + LORA_GATE
N_GROUPS = 4
EXPERTS_PER_GROUP = 8
N_EXPERTS = N_GROUPS * EXPERTS_PER_GROUP
TOP_K = 2
EXPERT_HIDDEN = 512
DEPTH = 1
ALPHA = (2.0 * DEPTH) ** 0.25
LN_EPS = 1e-5
RMS_EPS = 1e-5
RWKV_GN_EPS = 64e-5
LAM_INIT = 0.8 - 0.6 * math.exp(-0.3 * 0)

LANES = 128
ROUTE_LANES = 128
VMEM_LIMIT = 56 * 1024 * 1024

PROJ_TILE = 512
ATT_QB = 128
ATT_KB = 512
ATT_STEP_BLOCKS = 8
PRE_TILE = 256
CHUNK = 128
MIX_TILE = 256
OUT_TILE = 256
EXPERT_ROWS = 512
NEG_BIG = -1e30
LOG2E = math.log2(math.e)
Q_SCALE = ATT_QK_DIM ** -0.5 * LOG2E


def _bdot(a, b):
    return jnp.dot(a.astype(BF16), b.astype(BF16), preferred_element_type=F32)


def _bdot_nt(a, b):
    return lax.dot_general(a.astype(BF16), b.astype(BF16), (((1,), (1,)), ((), ())),
                           preferred_element_type=F32)


def _split3(x):
    hi = x.astype(BF16)
    r1 = x - hi.astype(F32)
    mid = r1.astype(BF16)
    lo = (r1 - mid.astype(F32)).astype(BF16)
    return hi, mid, lo


def _params(sem):
    return pltpu.CompilerParams(dimension_semantics=sem, vmem_limit_bytes=VMEM_LIMIT)


def _in_proj_kernel(x_ref, wqk_ref, wvt_ref, wrw_ref, qk_ref, vt_ref, rw_ref):
    xb = x_ref[0].astype(BF16)
    qk = jnp.dot(xb, wqk_ref[...], preferred_element_type=F32)
    qk_ref[0, :, :QK_WIDTH // 2] = (qk[:, :QK_WIDTH // 2] * Q_SCALE).astype(BF16)
    qk_ref[0, :, QK_WIDTH // 2:] = qk[:, QK_WIDTH // 2:].astype(BF16)
    vt_ref[0] = lax.dot_general(wvt_ref[...], xb, (((1,), (1,)), ((), ())),
                                preferred_element_type=F32).astype(BF16)
    rw_ref[0] = jnp.dot(xb, wrw_ref[...], preferred_element_type=F32)


def _in_proj(x, wqk, wvt, wrw):
    B, S, D = x.shape
    T = PROJ_TILE
    return pl.pallas_call(
        _in_proj_kernel,
        grid=(B, S // T),
        in_specs=[
            pl.BlockSpec((1, T, D), lambda b, i: (b, i, 0)),
            pl.BlockSpec((D, QK_WIDTH), lambda b, i: (0, 0)),
            pl.BlockSpec((ATT_WIDTH, D), lambda b, i: (0, 0)),
            pl.BlockSpec((D, RWKV_IN), lambda b, i: (0, 0)),
        ],
        out_specs=[
            pl.BlockSpec((1, T, QK_WIDTH), lambda b, i: (b, i, 0)),
            pl.BlockSpec((1, ATT_WIDTH, T), lambda b, i: (b, 0, i)),
            pl.BlockSpec((1, T, RWKV_IN), lambda b, i: (b, i, 0)),
        ],
        out_shape=[
            jax.ShapeDtypeStruct((B, S, QK_WIDTH), BF16),
            jax.ShapeDtypeStruct((B, ATT_WIDTH, S), BF16),
            jax.ShapeDtypeStruct((B, S, RWKV_IN), F32),
        ],
        compiler_params=_params(("parallel", "parallel")),
        name="in_proj",
    )(x, wqk, wvt, wrw)


def _stack_components(q):
    lane = lax.broadcasted_iota(jnp.int32, q.shape, 1)
    zero = jnp.zeros_like(q)
    return jnp.concatenate([jnp.where(lane < ATT_QK_DIM, q, zero),
                            jnp.where(lane >= ATT_QK_DIM, q, zero)], axis=0)


def _score_tile(k_ref, bias_ref, qq, win, j):
    KB = ATT_KB
    kt = k_ref[0, j * KB:(j + 1) * KB, :]
    s = lax.dot_general(kt, qq, (((1,), (1,)), ((), ())), preferred_element_type=F32)
    bias = bias_ref[pl.ds(win + j * KB, KB), :]
    return s - jnp.concatenate([bias, bias], axis=1)


def _scores_and_values(k_ref, vt_ref, bias_ref, qq_new, win_new, s_new, s_old, m_old, seq):
    QB, KB = ATT_QB, ATT_KB
    m_new = jnp.full((1, 2 * QB), NEG_BIG, F32)
    l = jnp.zeros((1, 2 * QB), F32)
    acc = jnp.zeros((ATT_V_DIM, 2 * QB), F32)
    for j in range(seq // KB):
        p = jnp.exp2(s_old[j * KB:(j + 1) * KB, :] - m_old)
        s = _score_tile(k_ref, bias_ref, qq_new, win_new, j)
        s_new[j * KB:(j + 1) * KB, :] = s
        m_new = jnp.maximum(m_new, jnp.max(s, axis=0, keepdims=True))
        l = l + jnp.sum(p, axis=0, keepdims=True)
        vt = vt_ref[0, :, j * KB:(j + 1) * KB]
        acc = acc + jnp.dot(vt, p.astype(BF16), preferred_element_type=F32)
    return m_new, l, acc


def _attention_kernel(slopes_ref, q_ref, qn_ref, k_ref, vt_ref, lam_ref, g_ref, o_ref,
                      bias_ref, sa_ref, sb_ref, ma_ref, *, seq):
    h = pl.program_id(1)
    i = pl.program_id(2)
    QB, KB = ATT_QB, ATT_KB
    nq = seq // QB
    slope = slopes_ref[h]

    @pl.when(i == 0)
    def _():
        for c in range(2 * seq // KB):
            r = lax.broadcasted_iota(jnp.int32, (KB, QB), 0) + (c * KB - seq)
            qq_ = lax.broadcasted_iota(jnp.int32, (KB, QB), 1)
            bias_ref[c * KB:(c + 1) * KB, :] = slope * jnp.abs(r - qq_).astype(F32)
        qq0 = _stack_components(q_ref[0, :QB, :])
        m0 = jnp.full((1, 2 * QB), NEG_BIG, F32)
        for j in range(seq // KB):
            s = _score_tile(k_ref, bias_ref, qq0, seq, j)
            sa_ref[j * KB:(j + 1) * KB, :] = s
            m0 = jnp.maximum(m0, jnp.max(s, axis=0, keepdims=True))
        ma_ref[...] = m0

    lmb = lam_ref[...]
    lam = (jnp.exp(jnp.sum(lmb[0:1] * lmb[1:2], axis=1, keepdims=True))
           - jnp.exp(jnp.sum(lmb[2:3] * lmb[3:4], axis=1, keepdims=True)) + LAM_INIT)

    def finish(l, acc):
        o = acc[:, :QB] / l[:, :QB] - lam * (acc[:, QB:] / l[:, QB:])
        ms = jnp.mean(o * o, axis=0, keepdims=True)
        o = o * lax.rsqrt(ms + RMS_EPS) * g_ref[...] * (1.0 - LAM_INIT)
        return o.T.astype(BF16)

    NB = ATT_STEP_BLOCKS
    bufs = (sa_ref, sb_ref)
    m_old = ma_ref[...]
    for p in range(NB):
        q_new = q_ref[0, (p + 1) * QB:(p + 2) * QB, :] if p + 1 < NB else qn_ref[0]
        win = pl.multiple_of(seq - jnp.minimum(NB * i + p + 1, nq - 1) * QB, QB)
        m_old, l, acc = _scores_and_values(k_ref, vt_ref, bias_ref, _stack_components(q_new), win,
                                           bufs[(p + 1) % 2], bufs[p % 2], m_old, seq)
        o_ref[0, p * QB:(p + 1) * QB, :] = finish(l, acc)
    ma_ref[...] = m_old


def _attention(qk, vt, att_lambda, subln_g):
    B, S, _ = qk.shape
    H = ATT_HEADS
    nq = S // ATT_QB
    NB = ATT_STEP_BLOCKS
    slopes = jnp.asarray([LOG2E * 2.0 ** (-8.0 * (i + 1) / H) for i in range(H)], F32)
    return pl.pallas_call(
        functools.partial(_attention_kernel, seq=S),
        grid_spec=pltpu.PrefetchScalarGridSpec(
            num_scalar_prefetch=1,
            grid=(B, H, nq // NB),
            in_specs=[
                pl.BlockSpec((1, NB * ATT_QB, LANES), lambda b, h, i, s: (b, i, h)),
                pl.BlockSpec((1, ATT_QB, LANES),
                             lambda b, h, i, s: (b, jnp.minimum(NB * i + NB, nq - 1), h)),
                pl.BlockSpec((1, S, LANES), lambda b, h, i, s: (b, 0, H + h)),
                pl.BlockSpec((1, ATT_V_DIM, S), lambda b, h, i, s: (b, h, 0)),
                pl.BlockSpec((4, ATT_QK_DIM), lambda b, h, i, s: (0, 0)),
                pl.BlockSpec((ATT_V_DIM, 1), lambda b, h, i, s: (0, 0)),
            ],
            out_specs=pl.BlockSpec((1, NB * ATT_QB, LANES), lambda b, h, i, s: (b, i, h)),
            scratch_shapes=[pltpu.VMEM((2 * S, ATT_QB), F32),
                            pltpu.VMEM((S, 2 * ATT_QB), F32),
                            pltpu.VMEM((S, 2 * ATT_QB), F32),
                            pltpu.VMEM((1, 2 * ATT_QB), F32)],
        ),
        out_shape=jax.ShapeDtypeStruct((B, S, ATT_WIDTH), BF16),
        compiler_params=_params(("parallel", "parallel", "arbitrary")),
        name="attention",
    )(slopes, qk, qk, qk, vt, att_lambda, subln_g.reshape(ATT_V_DIM, 1))


def _head_sum(x, blk):
    hi, mid, lo = _split3(x)
    return (jnp.dot(hi, blk, preferred_element_type=F32) + jnp.dot(mid, blk, preferred_element_type=F32)
            + jnp.dot(lo, blk, preferred_element_type=F32))


def _rwkv_pre_kernel(p_ref, prev_ref, next_ref, mu_ref, w0_ref, wup_ref, a0_ref, aup_ref, gup_ref,
                     kk_ref, ka_ref, rk_ref, blk_ref,
                     r_ref, a_ref, lw_ref, kd_ref, kka_ref, vt_ref, g_ref, bonus_ref):
    i = pl.program_id(1)
    n = pl.num_programs(1)
    T = PRE_TILE
    W = RWKV_WIDTH
    p = p_ref[0]
    row = lax.broadcasted_iota(jnp.int32, p.shape, 0)
    prev_row = jnp.where(i > 0, prev_ref[0, 7:8, :], 0.0)
    next_row = jnp.where(i < n - 1, next_ref[0, 0:1, :], 0.0)
    prev = jnp.where(row == 0, prev_row, pltpu.roll(p, 1, axis=0))
    nxt = jnp.where(row == T - 1, next_row, pltpu.roll(p, T - 1, axis=0))
    p = p + mu_ref[0:1, :] * (prev - p) + mu_ref[1:2, :] * (nxt - p)

    r = p[:, 0:W]
    k = p[:, W:2 * W]
    v = p[:, 2 * W:3 * W]
    c = 3 * W
    wd = p[:, c:c + 2 * LORA_DECAY]
    ad = p[:, c + 2 * LORA_DECAY:c + 2 * LORA_DECAY + 2 * LORA_AAA]
    gd = p[:, c + 2 * LORA_DECAY + 2 * LORA_AAA:]

    wl = w0_ref[...] + _bdot(jnp.tanh(wd), wup_ref[...])
    lw = -math.exp(-0.5) * jax.nn.sigmoid(wl)
    av = jax.nn.sigmoid(a0_ref[...] + _bdot(ad, aup_ref[...]))
    g_ref[0] = _bdot(jax.nn.sigmoid(gd), gup_ref[...])

    blk = blk_ref[...]
    kkr = k * kk_ref[...]
    kk = kkr / jnp.maximum(jnp.sqrt(_head_sum(kkr * kkr, blk)), 1e-12)
    ksum = jnp.zeros_like(k)
    for d in range(2):
        a_d = av[:, d * W:(d + 1) * W]
        k_d = k * (1.0 + (a_d - 1.0) * ka_ref[...])
        ksum = ksum + k_d
        lw_ref[d, 0] = lw[:, d * W:(d + 1) * W]
        kd_ref[d, 0] = k_d
        kka_ref[d, 0] = kk * a_d
    r_ref[0] = r
    a_ref[0] = -kk
    vt_ref[0] = v.T
    bonus_ref[0] = _head_sum(r * ksum * rk_ref[...], blk) * v


def _rwkv_pre(rw, mu, w0cat, wup_blk, a0cat, aup_blk, gup, k_k, k_a, r_k, blk):
    B, S, _ = rw.shape
    T = PRE_TILE
    W = RWKV_WIDTH
    nb8 = S // 8
    full = lambda shape: pl.BlockSpec(shape, lambda b, i: (0,) * len(shape))
    row_spec = pl.BlockSpec((1, T, W), lambda b, i: (b, i, 0))
    dir_spec = pl.BlockSpec((2, 1, T, W), lambda b, i: (0, b, i, 0))
    row_shape = jax.ShapeDtypeStruct((B, S, W), F32)
    dir_shape = jax.ShapeDtypeStruct((2, B, S, W), F32)
    return pl.pallas_call(
        _rwkv_pre_kernel,
        grid=(B, S // T),
        in_specs=[
            pl.BlockSpec((1, T, RWKV_IN), lambda b, i: (b, i, 0)),
            pl.BlockSpec((1, 8, RWKV_IN), lambda b, i: (b, jnp.maximum(i * (T // 8) - 1, 0), 0)),
            pl.BlockSpec((1, 8, RWKV_IN), lambda b, i: (b, jnp.minimum((i + 1) * (T // 8), nb8 - 1), 0)),
            full((2, RWKV_IN)), full((1, 2 * W)), full((2 * LORA_DECAY, 2 * W)),
            full((1, 2 * W)), full((2 * LORA_AAA, 2 * W)), full((LORA_GATE, W)),
            full((1, W)), full((1, W)), full((1, W)), full((W, W)),
        ],
        out_specs=[row_spec, row_spec, dir_spec, dir_spec, dir_spec,
                   pl.BlockSpec((1, W, T), lambda b, i: (b, 0, i)), row_spec, row_spec],
        out_shape=[row_shape, row_shape, dir_shape, dir_shape, dir_shape,
                   jax.ShapeDtypeStruct((B, W, S), F32), row_shape, row_shape],
        compiler_params=_params(("parallel", "parallel")),
        name="rwkv_pre",
    )(rw, rw, rw, mu, w0cat, wup_blk, a0cat, aup_blk, gup, k_k, k_a, r_k, blk)


def _rwkv_scan_kernel(r_ref, a_ref, lw_ref, kd_ref, kka_ref, vt_ref, yt_ref, state_ref):
    d = pl.program_id(1)
    c = pl.program_id(2)
    C = CHUNK
    N = RWKV_HEAD_DIM

    @pl.when(c == 0)
    def _():
        state_ref[...] = jnp.zeros_like(state_ref)

    row = lax.broadcasted_iota(jnp.int32, (C, C), 0)
    col = lax.broadcasted_iota(jnp.int32, (C, C), 1)
    order = (row - col) * (1 - 2 * d)
    strict = order > 0
    incl = order >= 0
    tri = jnp.where(incl, 1.0, 0.0).astype(BF16)
    eye = jnp.where(row == col, 1.0, 0.0)

    lw = lw_ref[0, 0]
    k = kd_ref[0, 0]
    b = kka_ref[0, 0]
    lw_hi, lw_mid, lw_lo = _split3(lw)
    cs = (jnp.dot(tri, lw_hi, preferred_element_type=F32) + jnp.dot(tri, lw_mid, preferred_element_type=F32)
          + jnp.dot(tri, lw_lo, preferred_element_type=F32))
    tot = jnp.sum(lw, axis=0, keepdims=True)
    g_inv = jnp.exp(-cs)
    g_last = jnp.exp(tot - cs)
    g_tot = jnp.exp(tot)
    a_t = (a_ref[0] * jnp.exp(cs - lw)).astype(BF16)
    r_t = (r_ref[0] * jnp.exp(cs)).astype(BF16)
    lhs = jnp.concatenate([a_t, r_t], axis=0)
    rhs = jnp.concatenate([(b * g_inv).astype(BF16), (k * g_inv).astype(BF16)], axis=0)
    b_end = (b * g_last).astype(BF16)
    k_end = (k * g_last).astype(BF16)

    heads = range(RWKV_HEADS)
    sl = [slice(h * N, (h + 1) * N) for h in heads]
    vt = [vt_ref[0, sl[h], :].astype(BF16) for h in heads]
    s0 = [state_ref[h] for h in heads]
    gram = [_bdot_nt(lhs[:, sl[h]], rhs[:, sl[h]]) for h in heads]
    l_ab = [jnp.where(strict, gram[h][:C, :C], 0.0) for h in heads]
    l_ak = [jnp.where(strict, gram[h][:C, C:], 0.0).astype(BF16) for h in heads]
    m_rb = [jnp.where(incl, gram[h][C:, :C], 0.0).astype(BF16) for h in heads]
    m_rk = [jnp.where(incl, gram[h][C:, C:], 0.0).astype(BF16) for h in heads]

    levels = int(math.log2(C))
    t = [eye + l_ab[h] for h in heads]
    pw = [_bdot(l_ab[h], l_ab[h]) for h in heads]
    for j in range(1, levels):
        pb = [pw[h].astype(BF16) for h in heads]
        if j < levels - 1:
            both = [_bdot(pb[h], jnp.concatenate([pb[h], t[h].astype(BF16)], axis=1)) for h in heads]
            pw = [both[h][:, :C] for h in heads]
            t = [t[h] + both[h][:, C:] for h in heads]
        else:
            t = [t[h] + _bdot(pb[h], t[h]) for h in heads]

    tb = [t[h].astype(BF16) for h in heads]
    xw = [_bdot(tb[h], jnp.concatenate([l_ak[h], a_t[:, sl[h]]], axis=1)) for h in heads]
    s0b = [s0[h].astype(BF16) for h in heads]
    ut = [_bdot_nt(jnp.concatenate([vt[h], s0b[h]], axis=1), xw[h]).astype(BF16)
          for h in heads]
    for h in heads:
        yt_ref[0, 0, sl[h], :] = _bdot_nt(
            jnp.concatenate([ut[h], vt[h], s0b[h]], axis=1),
            jnp.concatenate([m_rb[h], m_rk[h], r_t[:, sl[h]]], axis=1))
    for h in heads:
        state_ref[h] = s0[h] * g_tot[:, sl[h]] + _bdot(
            jnp.concatenate([ut[h], vt[h]], axis=1),
            jnp.concatenate([b_end[:, sl[h]], k_end[:, sl[h]]], axis=0))


def _rwkv_scan(r, a, lw, kd, kka, vt):
    B, S, W = r.shape
    C = CHUNK
    nc = S // C
    cidx = lambda d, c: c + d * (nc - 1 - 2 * c)
    row_spec = pl.BlockSpec((1, C, W), lambda b, d, c: (b, cidx(d, c), 0))
    dir_spec = pl.BlockSpec((1, 1, C, W), lambda b, d, c: (d, b, cidx(d, c), 0))
    return pl.pallas_call(
        _rwkv_scan_kernel,
        grid=(B, 2, nc),
        in_specs=[row_spec, row_spec, dir_spec, dir_spec, dir_spec,
                  pl.BlockSpec((1, W, C), lambda b, d, c: (b, 0, cidx(d, c)))],
        out_specs=pl.BlockSpec((1, 1, W, C), lambda b, d, c: (d, b, 0, cidx(d, c))),
        out_shape=jax.ShapeDtypeStruct((2, B, W, S), F32),
        scratch_shapes=[pltpu.VMEM((RWKV_HEADS, RWKV_HEAD_DIM, RWKV_HEAD_DIM), F32)],
        compiler_params=_params(("parallel", "parallel", "arbitrary")),
        name="rwkv_scan",
    )(r, a, lw, kd, kka, vt)


def _layernorm(z, g, b):
    mu = jnp.mean(z, axis=-1, keepdims=True)
    zc = z - mu
    var = jnp.mean(zc * zc, axis=-1, keepdims=True)
    return zc * lax.rsqrt(var + LN_EPS) * g + b


def _first_lane_where(mask, lane):
    return jnp.min(jnp.where(mask, lane, ROUTE_LANES), axis=1, keepdims=True)


def _mix_ln1_kernel(x_ref, att_ref, ytf_ref, ytb_ref, g_ref, bonus_ref, gng_ref, gnb_ref,
                    woa_ref, wob_ref, l1g_ref, l1b_ref, wrh_ref, wrl_ref, br_ref,
                    h_ref, exp_ref, gate_ref, cnt_ref, base_ref):
    T = MIX_TILE
    N = RWKV_HEAD_DIM
    yt = ytf_ref[0, 0] + ytb_ref[0, 0]
    y3 = yt.reshape(RWKV_HEADS, N, T)
    mu = jnp.mean(y3, axis=1, keepdims=True)
    yc = y3 - mu
    var = jnp.mean(yc * yc, axis=1, keepdims=True)
    yn = (yc * lax.rsqrt(var + RWKV_GN_EPS)).reshape(RWKV_WIDTH, T)
    yn = yn * gng_ref[...] + gnb_ref[...]
    tm = (yn.T + bonus_ref[0]) * g_ref[0]
    mix = (jnp.dot(att_ref[0], woa_ref[...], preferred_element_type=F32)
           + _bdot(tm, wob_ref[...]))
    h = _layernorm(ALPHA * x_ref[0] + mix, l1g_ref[...], l1b_ref[...])
    h_ref[0] = h

    h_hi = h.astype(BF16)
    h_lo = (h - h_hi.astype(F32)).astype(BF16)
    logits = (jnp.dot(h_hi, wrh_ref[...], preferred_element_type=F32)
              + jnp.dot(h_lo, wrh_ref[...], preferred_element_type=F32)
              + jnp.dot(h_hi, wrl_ref[...], preferred_element_type=F32)) + br_ref[...]
    lane = lax.broadcasted_iota(jnp.int32, (T, ROUTE_LANES), 1)
    neg = jnp.asarray(-jnp.inf, F32)
    gl = jnp.where(lane < N_GROUPS, logits, neg)
    gmax = jnp.max(gl, axis=1, keepdims=True)
    grp = _first_lane_where(gl == gmax, lane)
    g1 = 1.0 / jnp.sum(jnp.exp(gl - gmax), axis=1, keepdims=True)
    lo = N_GROUPS + EXPERTS_PER_GROUP * grp
    sel = jnp.where((lane >= lo) & (lane < lo + EXPERTS_PER_GROUP), logits, neg)
    v1 = jnp.max(sel, axis=1, keepdims=True)
    i1 = _first_lane_where(sel == v1, lane)
    sel2 = jnp.where(lane == i1, neg, sel)
    v2 = jnp.max(sel2, axis=1, keepdims=True)
    i2 = _first_lane_where(sel2 == v2, lane)
    e2 = jnp.exp(v2 - v1)
    den = 1.0 + e2
    gate1 = g1 * (1.0 / den)
    gate2 = g1 * (e2 / den)
    gate_ref[0] = jnp.where(lane == 0, gate1, jnp.where(lane == 1, gate2, 0.0))

    @pl.when((pl.program_id(0) == 0) & (pl.program_id(1) == 0))
    def _():
        base_ref[...] = jnp.zeros_like(base_ref)

    e1 = i1 - N_GROUPS
    e2 = i2 - N_GROUPS
    oh1 = jnp.where(lane == e1, 1.0, 0.0)
    oh2 = jnp.where(lane == e2, 1.0, 0.0)
    cnt = oh1 + oh2
    ti = lax.broadcasted_iota(jnp.int32, (T, T), 0)
    tj = lax.broadcasted_iota(jnp.int32, (T, T), 1)
    earlier = jnp.where(tj < ti, 1.0, 0.0).astype(BF16)
    before = jnp.dot(earlier, cnt.astype(BF16), preferred_element_type=F32) + base_ref[...]
    rank1 = jnp.sum(oh1 * before, axis=1, keepdims=True).astype(jnp.int32)
    rank2 = jnp.sum(oh2 * before, axis=1, keepdims=True).astype(jnp.int32)
    total = base_ref[...] + jnp.sum(cnt, axis=0, keepdims=True)
    base_ref[...] = total
    cnt_ref[...] = total
    exp_ref[0] = jnp.where(lane == 0, e1, jnp.where(lane == 1, e2, jnp.where(
        lane == 2, rank1, jnp.where(lane == 3, rank2, 0))))


def _mix_ln1(x, att, yt, g, bonus, gn_g, gn_b, woa, wob, l1g, l1b, wr_hi, wr_lo, br):
    B, S, D = x.shape
    T = MIX_TILE
    W = RWKV_WIDTH
    full = lambda shape: pl.BlockSpec(shape, lambda b, i: (0,) * len(shape))
    row = lambda width: pl.BlockSpec((1, T, width), lambda b, i: (b, i, 0))
    return pl.pallas_call(
        _mix_ln1_kernel,
        grid=(B, S // T),
        in_specs=[
            row(D), row(ATT_WIDTH),
            pl.BlockSpec((1, 1, W, T), lambda b, i: (0, b, 0, i)),
            pl.BlockSpec((1, 1, W, T), lambda b, i: (1, b, 0, i)),
            row(W), row(W), full((W, 1)), full((W, 1)),
            full((ATT_WIDTH, D)), full((W, D)), full((1, D)), full((1, D)),
            full((D, ROUTE_LANES)), full((D, ROUTE_LANES)), full((1, ROUTE_LANES)),
        ],
        out_specs=[row(D), row(ROUTE_LANES), row(ROUTE_LANES), full((1, ROUTE_LANES))],
        out_shape=[jax.ShapeDtypeStruct((B, S, D), F32),
                   jax.ShapeDtypeStruct((B, S, ROUTE_LANES), jnp.int32),
                   jax.ShapeDtypeStruct((B, S, ROUTE_LANES), F32),
                   jax.ShapeDtypeStruct((1, ROUTE_LANES), F32)],
        scratch_shapes=[pltpu.VMEM((1, ROUTE_LANES), F32)],
        compiler_params=_params(("arbitrary", "arbitrary")),
        name="mix_ln1",
    )(x, att, yt, yt, g, bonus, gn_g, gn_b, woa, wob, l1g, l1b, wr_hi, wr_lo, br)


def _start_row_gather(idx_ref, src_hbm, dst, sem, rows):
    for r in range(rows):
        pltpu.make_async_copy(src_hbm.at[pl.ds(idx_ref[0, 0, r], 1), :],
                              dst.at[pl.ds(r, 1), :], sem).start()


def _wait_row_gather(src_hbm, dst, sem, rows):
    pltpu.make_async_copy(src_hbm.at[pl.ds(0, rows), :], dst, sem).wait()


def _dispatch_kernel(pos_ref, h_ref, xs_in_hbm, xs_hbm, sem):
    del xs_in_hbm
    T = OUT_TILE
    for k in range(TOP_K):
        for r in range(T):
            pltpu.make_async_copy(h_ref.at[pl.ds(r, 1), :],
                                  xs_hbm.at[pl.ds(pos_ref[0, 0, k * T + r], 1), :], sem).start()
    for k in range(TOP_K):
        pltpu.make_async_copy(h_ref, xs_hbm.at[pl.ds(0, T), :], sem).wait()


def _dispatch(h_flat, pos_tiles, n_rows):
    n_tok, D = h_flat.shape
    T = OUT_TILE
    nt = n_tok // T
    return pl.pallas_call(
        _dispatch_kernel,
        grid=(nt,),
        in_specs=[
            pl.BlockSpec((1, 1, TOP_K * T), lambda i: (i, 0, 0), memory_space=pltpu.SMEM),
            pl.BlockSpec((T, D), lambda i: (i, 0)),
            pl.BlockSpec(memory_space=pl.ANY),
        ],
        out_specs=pl.BlockSpec(memory_space=pl.ANY),
        out_shape=jax.ShapeDtypeStruct((n_rows, D), F32),
        scratch_shapes=[pltpu.SemaphoreType.DMA(())],
        input_output_aliases={2: 0},
        compiler_params=_params(("arbitrary",)),
        name="dispatch",
    )(pos_tiles, h_flat, jnp.zeros((n_rows, D), F32))


def _experts_kernel(bexp_ref, x_ref, wg_ref, wu_ref, wd_ref, y_ref):
    del bexp_ref
    xb = x_ref[...].astype(BF16)
    hid = (jax.nn.silu(jnp.dot(xb, wg_ref[0], preferred_element_type=F32))
           * jnp.dot(xb, wu_ref[0], preferred_element_type=F32))
    y_ref[...] = _bdot(hid, wd_ref[0])


def _experts(xs, block_exp, wg, wu, wd):
    D = D_MODEL
    R = EXPERT_ROWS
    nb = block_exp.shape[0]
    return pl.pallas_call(
        _experts_kernel,
        grid_spec=pltpu.PrefetchScalarGridSpec(
            num_scalar_prefetch=1,
            grid=(nb,),
            in_specs=[
                pl.BlockSpec((R, D), lambda i, e: (i, 0)),
                pl.BlockSpec((1, D, EXPERT_HIDDEN), lambda i, e: (e[i], 0, 0)),
                pl.BlockSpec((1, D, EXPERT_HIDDEN), lambda i, e: (e[i], 0, 0)),
                pl.BlockSpec((1, EXPERT_HIDDEN, D), lambda i, e: (e[i], 0, 0)),
            ],
            out_specs=pl.BlockSpec((R, D), lambda i, e: (i, 0)),
        ),
        out_shape=jax.ShapeDtypeStruct((nb * R, D), F32),
        compiler_params=_params(("parallel",)),
        name="experts",
    )(block_exp, xs, wg, wu, wd)


def _combine_ln2_kernel(pos_ref, pos_next_ref, h_ref, gate_ref, y_hbm, l2g_ref, l2b_ref, o_ref,
                        ybuf, sem):
    i = pl.program_id(0)
    n = pl.num_programs(0)
    T = OUT_TILE
    R = TOP_K * T
    slot = i % 2

    @pl.when(i == 0)
    def _():
        _start_row_gather(pos_ref, y_hbm, ybuf.at[0], sem.at[0], R)

    @pl.when(i + 1 < n)
    def _():
        _start_row_gather(pos_next_ref, y_hbm, ybuf.at[1 - slot], sem.at[1 - slot], R)

    _wait_row_gather(y_hbm, ybuf.at[slot], sem.at[slot], R)
    gates = gate_ref[...]
    moe = ybuf[slot, 0:T, :] * gates[:, 0:1] + ybuf[slot, T:R, :] * gates[:, 1:2]
    o_ref[...] = _layernorm(ALPHA * h_ref[...] + moe, l2g_ref[...], l2b_ref[...])


def _combine_ln2(h_flat, gates, pos, yb, l2g, l2b):
    n_tok, D = h_flat.shape
    T = OUT_TILE
    nt = n_tok // T
    idx_spec = lambda f: pl.BlockSpec((1, 1, TOP_K * T), f, memory_space=pltpu.SMEM)
    return pl.pallas_call(
        _combine_ln2_kernel,
        grid=(nt,),
        in_specs=[
            idx_spec(lambda i: (i, 0, 0)),
            idx_spec(lambda i: (jnp.minimum(i + 1, nt - 1), 0, 0)),
            pl.BlockSpec((T, D), lambda i: (i, 0)),
            pl.BlockSpec((T, ROUTE_LANES), lambda i: (i, 0)),
            pl.BlockSpec(memory_space=pl.ANY),
            pl.BlockSpec((1, D), lambda i: (0, 0)),
            pl.BlockSpec((1, D), lambda i: (0, 0)),
        ],
        out_specs=pl.BlockSpec((T, D), lambda i: (i, 0)),
        out_shape=jax.ShapeDtypeStruct((n_tok, D), F32),
        scratch_shapes=[pltpu.VMEM((2, TOP_K * T, D), F32), pltpu.SemaphoreType.DMA((2,))],
        compiler_params=_params(("arbitrary",)),
        name="combine_ln2",
    )(pos, pos, h_flat, gates, yb, l2g, l2b)


def _dispatch_plan(expert, rank, counts):
    n_tok = expert.shape[0]
    R = EXPERT_ROWS
    padded = (counts + R - 1) // R * R
    pends = jnp.cumsum(padded)
    pstarts = pends - padded
    ids = jnp.arange(N_EXPERTS, dtype=jnp.int32)
    pos = rank + jnp.sum(jnp.where(expert[..., None] == ids, pstarts, 0), axis=-1)
    nb = n_tok * TOP_K // R + N_EXPERTS
    block_start = jnp.arange(nb, dtype=jnp.int32) * R
    block_exp = jnp.minimum(jnp.sum(block_start[:, None] >= pends[None, :], axis=1),
                            N_EXPERTS - 1).astype(jnp.int32)
    return pos.astype(jnp.int32), block_exp


def _trunk(x, w):
    B, S, D = x.shape
    qk, vt, rw = _in_proj(x, w["wqk"], w["wvt"], w["wrw"])
    att = _attention(qk, vt, w["att_lambda"], w["subln_g"])
    r, a, lw, kd, kka, rvt, g, bonus = _rwkv_pre(
        rw, w["mu"], w["w0cat"], w["wup_blk"], w["a0cat"], w["aup_blk"], w["gup"],
        w["k_k"], w["k_a"], w["r_k"], w["blk"])
    yt = _rwkv_scan(r, a, lw, kd, kka, rvt)
    h, route, gates, counts = _mix_ln1(x, att, yt, g, bonus, w["gn_g"], w["gn_b"], w["woa"], w["wob"],
                                       w["l1g"], w["l1b"], w["wr_hi"], w["wr_lo"], w["br"])
    n_tok = B * S
    h_flat = h.reshape(n_tok, D)
    route = route.reshape(n_tok, ROUTE_LANES)
    pos, block_exp = _dispatch_plan(route[:, :TOP_K], route[:, TOP_K:2 * TOP_K],
                                    counts[0, :N_EXPERTS].astype(jnp.int32))
    T = OUT_TILE
    pos_tiles = pos.reshape(n_tok // T, T, TOP_K).transpose(0, 2, 1).reshape(n_tok // T, 1, TOP_K * T)
    xs = _dispatch(h_flat, pos_tiles, block_exp.shape[0] * EXPERT_ROWS)
    yb = _experts(xs, block_exp, w["wg"], w["wu"], w["wd"])
    out = _combine_ln2(h_flat, gates.reshape(n_tok, ROUTE_LANES), pos_tiles, yb, w["l2g"], w["l2b"])
    return out.reshape(B, S, D)


def _block_diag2(m):
    z = jnp.zeros_like(m[0])
    return jnp.concatenate([jnp.concatenate([m[0], z], axis=1),
                            jnp.concatenate([z, m[1]], axis=1)], axis=0)


def kernel(x_prompt, x_sample, w_in, att_lambda, att_subln_g, rwkv_mu, rwkv_w0, rwkv_w_up, rwkv_a0, rwkv_a_up, rwkv_g_up, rwkv_k_k, rwkv_k_a, rwkv_r_k, rwkv_ln_g, rwkv_ln_b, w_out, ln1_g, ln1_b, router_g_w, router_g_b, router_e_w, router_e_b, exp_w_gate, exp_w_up, exp_w_down, ln2_g, ln2_b):
    assert w_in.shape[0] == DEPTH
    W = RWKV_WIDTH
    D = D_MODEL
    win = w_in[0]
    head = jnp.arange(W, dtype=jnp.int32) // RWKV_HEAD_DIM
    n_route = N_GROUPS + N_EXPERTS
    wr = jnp.concatenate([router_g_w[0],
                          jnp.transpose(router_e_w[0], (1, 0, 2)).reshape(D, N_EXPERTS)], axis=1)
    br = jnp.concatenate([router_g_b[0], router_e_b[0].reshape(N_EXPERTS)])
    wr_pad = jnp.pad(wr, ((0, 0), (0, ROUTE_LANES - n_route)))
    wr_hi = wr_pad.astype(BF16)
    w = {
        "wqk": win[:, :QK_WIDTH].astype(BF16),
        "wvt": win[:, QK_WIDTH:QK_WIDTH + ATT_WIDTH].T.astype(BF16),
        "wrw": win[:, QK_WIDTH + ATT_WIDTH:].astype(BF16),
        "att_lambda": att_lambda[0],
        "subln_g": att_subln_g[0],
        "mu": rwkv_mu[0],
        "w0cat": rwkv_w0[0].reshape(1, 2 * W),
        "wup_blk": _block_diag2(rwkv_w_up[0]).astype(BF16),
        "a0cat": rwkv_a0[0].reshape(1, 2 * W),
        "aup_blk": _block_diag2(rwkv_a_up[0]).astype(BF16),
        "gup": rwkv_g_up[0].astype(BF16),
        "k_k": rwkv_k_k[0].reshape(1, W),
        "k_a": rwkv_k_a[0].reshape(1, W),
        "r_k": rwkv_r_k[0].reshape(1, W),
        "blk": (head[:, None] == head[None, :]).astype(BF16),
        "gn_g": rwkv_ln_g[0].reshape(W, 1),
        "gn_b": rwkv_ln_b[0].reshape(W, 1),
        "woa": w_out[0][:ATT_WIDTH].astype(BF16),
        "wob": w_out[0][ATT_WIDTH:].astype(BF16),
        "l1g": ln1_g[0].reshape(1, D),
        "l1b": ln1_b[0].reshape(1, D),
        "wr_hi": wr_hi,
        "wr_lo": (wr_pad - wr_hi.astype(F32)).astype(BF16),
        "br": jnp.pad(br, (0, ROUTE_LANES - n_route)).reshape(1, ROUTE_LANES),
        "wg": exp_w_gate[0].astype(BF16),
        "wu": exp_w_up[0].astype(BF16),
        "wd": exp_w_down[0].astype(BF16),
        "l2g": ln2_g[0].reshape(1, D),
        "l2b": ln2_b[0].reshape(1, D),
    }
    return (_trunk(x_prompt, w), _trunk(x_sample, w))
```
